```python
import math, functools
import jax, jax.numpy as jnp
from jax import lax
import numpy as np

D_MODEL = 1024
BATCH = 32
SEQ = 2048
DEPTH = 1

EPS = 1e-6
D_FF = 2816
MIX_WIDTH = D_MODEL
POOL_WIDTH = MIX_WIDTH // 2
N_POOL_GROUPS = 4
POOL_GROUP_DIM = POOL_WIDTH // N_POOL_GROUPS
POOL_WINDOWS = (2, 4, 8, 16)
ATTN_WIDTH = MIX_WIDTH // 2
N_HEADS = 8
HEAD_DIM = ATTN_WIDTH // N_HEADS
MOBA_BLOCK = 256
MOBA_TOPK = 3
Q_BLOCK = 128
N_BRANCHES = 2
IN_WIDTH = POOL_WIDTH + 3 * ATTN_WIDTH + N_BRANCHES * D_MODEL

kernel_name = "hybrid_pool_moba_gated_macaron"


def rms_norm(x, g):
    xf = x.astype(jnp.float32)
    y = xf * lax.rsqrt(jnp.mean(xf * xf, axis=-1, keepdims=True) + EPS)
    return (y * g.astype(jnp.float32)).astype(x.dtype)


def swiglu(x, w_gate, w_up, w_down):
    return (jax.nn.silu(x @ w_gate) * (x @ w_up)) @ w_down


def alibi_slopes(n_heads):
    return jnp.exp2(-8.0 * jnp.arange(1, n_heads + 1, dtype=jnp.float32) / n_heads)


def multiscale_pool(u, w_pool, pool_scale):
    B, S, _ = u.shape
    ug = u.reshape(B, S, N_POOL_GROUPS, POOL_GROUP_DIM)
    csum = jnp.cumsum(ug.astype(jnp.float32), axis=1)
    csum = jnp.pad(csum, ((0, 0), (1, 0), (0, 0), (0, 0)))
    t = jnp.arange(S)
    pooled = []
    for g, w in enumerate(POOL_WINDOWS):
        lo = jnp.maximum(t + 1 - w, 0)
        win_sum = csum[:, 1:, g] - csum[:, lo, g]
        count = (t + 1 - lo).astype(jnp.float32)
        pooled.append(win_sum / count[None, :, None])
    pooled = jnp.stack(pooled, axis=2)
    mixed = (pooled - ug.astype(jnp.float32)).astype(u.dtype)
    y = jnp.einsum('bsgc,gcd->bsgd', mixed, w_pool).reshape(B, S, POOL_WIDTH)
    return y * pool_scale


def moba_attention(q, k, v, slopes):
    B, S, H, Dh = q.shape
    BS = MOBA_BLOCK
    nb = -(-S // BS)
    pad = nb * BS - S
    qh = q.transpose(0, 2, 1, 3)
    kh = jnp.pad(k.transpose(0, 2, 1, 3), ((0, 0), (0, 0), (0, pad), (0, 0)))
    vh = jnp.pad(v.transpose(0, 2, 1, 3), ((0, 0), (0, 0), (0, pad), (0, 0)))
    kb = kh.reshape(B, H, nb, BS, Dh)
    vb = vh.reshape(B, H, nb, BS, Dh)

    t_all = jnp.arange(S)
    k_mean = jnp.mean(kb.astype(jnp.float32), axis=3)
    gate = jnp.einsum('bhsd,bhnd->bhsn', qh.astype(jnp.float32), k_mean)
    fully_past = jnp.arange(nb)[None, :] < (t_all // BS)[:, None]
    gate = jnp.where(fully_past[None, None], gate, -jnp.inf)
    topk = min(MOBA_TOPK, nb)
    _, sel_idx = lax.top_k(gate, topk)

    nqb = S // Q_BLOCK
    q_blocks = qh.reshape(B, H, nqb, Q_BLOCK, Dh).transpose(0, 2, 1, 3, 4).reshape(B * nqb, H, Q_BLOCK, Dh)
    idx_blocks = sel_idx.reshape(B, H, nqb, Q_BLOCK, topk).transpose(0, 2, 1, 3, 4).reshape(B * nqb, H, Q_BLOCK, topk)
    b_ids = jnp.repeat(jnp.arange(B), nqb)
    qb_ids = jnp.tile(jnp.arange(nqb), B)
    scale = Dh ** -0.5
    head_ix = jnp.arange(H)[:, None, None]
    pos_in_blk = jnp.arange(BS)

    def one_query_block(args):
        b, qi, qblk, idx = args
        kb_b = kb[b]
        vb_b = vb[b]
        tq = qi * Q_BLOCK + jnp.arange(Q_BLOCK)
        own = (qi * Q_BLOCK) // BS
        k_own = lax.dynamic_index_in_dim(kb_b, own, axis=1, keepdims=False)
        v_own = lax.dynamic_index_in_dim(vb_b, own, axis=1, keepdims=False)
        s_own = own * BS + pos_in_blk
        k_sel = kb_b[head_ix, idx]
        v_sel = vb_b[head_ix, idx]
        s_sel = idx[..., None] * BS + pos_in_blk
        slot_ok = jnp.arange(topk)[None, :] < (tq // BS)[:, None]

        l_sel = (jnp.einsum('hqd,hqjsd->hqjs', qblk, k_sel).astype(jnp.float32) * scale
                 - slopes[:, None, None, None] * (tq[None, :, None, None] - s_sel).astype(jnp.float32))
        l_sel = jnp.where(slot_ok[None, :, :, None], l_sel, -jnp.inf)
        l_own = (jnp.einsum('hqd,hsd->hqs', qblk, k_own).astype(jnp.float32) * scale
                 - slopes[:, None, None] * (tq[:, None] - s_own[None, :]).astype(jnp.float32))
        l_own = jnp.where((s_own[None, :] <= tq[:, None])[None], l_own, -jnp.inf)

        logits = jnp.concatenate([l_sel.reshape(H, Q_BLOCK, topk * BS), l_own], axis=-1)
        p = jax.nn.softmax(logits, axis=-1).astype(v.dtype)
        p_sel = p[..., :topk * BS].reshape(H, Q_BLOCK, topk, BS)
        p_own = p[..., topk * BS:]
        return (jnp.einsum('hqjs,hqjsd->hqd', p_sel, v_sel)
                + jnp.einsum('hqs,hsd->hqd', p_own, v_own))

    out = lax.map(one_query_block, (b_ids, qb_ids, q_blocks, idx_blocks))
    return out.reshape(B, nqb, H, Q_BLOCK, Dh).transpose(0, 1, 3, 2, 4).reshape(B, S, H * Dh)


def setup_inputs(seed: int = 0) -> dict:
    key = jax.random.key(seed)
    ks = jax.random.split(key, 17)
    L = DEPTH
    f32 = jnp.float32

    def w(k, shape, fan_in):
        return jax.random.normal(k, shape, f32) * fan_in ** -0.5

    def gain(k, shape):
        return 1.0 + 0.1 * jax.random.normal(k, shape, f32)

    return {
        "x": jax.random.normal(ks[0], (BATCH, SEQ, D_MODEL), f32),
        "ffn1_norm": gain(ks[1], (L, D_MODEL)),
        "ffn1_w_gate": w(ks[2], (L, D_MODEL, D_FF), D_MODEL),
        "ffn1_w_up": w(ks[3], (L, D_MODEL, D_FF), D_MODEL),
        "ffn1_w_down": w(ks[4], (L, D_FF, D_MODEL), D_FF),
        "mix_norm": gain(ks[5], (L, D_MODEL)),
        "w_in": w(ks[6], (L, D_MODEL, IN_WIDTH), D_MODEL),
        "pool_w": w(ks[7], (L, N_POOL_GROUPS, POOL_GROUP_DIM, POOL_GROUP_DIM), POOL_GROUP_DIM),
        "pool_scale": gain(ks[8], (L, POOL_WIDTH)),
        "w_branch_pool": w(ks[9], (L, POOL_WIDTH, D_MODEL), POOL_WIDTH),
        "w_branch_attn": w(ks[10], (L, ATTN_WIDTH, D_MODEL), ATTN_WIDTH),
        "w_out": w(ks[11], (L, D_MODEL, D_MODEL), D_MODEL),
        "ffn2_norm": gain(ks[12], (L, D_MODEL)),
        "ffn2_w_gate": w(ks[13], (L, D_MODEL, D_FF), D_MODEL),
        "ffn2_w_up": w(ks[14], (L, D_MODEL, D_FF), D_MODEL),
        "ffn2_w_down": w(ks[15], (L, D_FF, D_MODEL), D_FF),
        "final_norm": gain(ks[16], (D_MODEL,)),
    }


def reference(x, ffn1_norm, ffn1_w_gate, ffn1_w_up, ffn1_w_down, mix_norm, w_in,
              pool_w, pool_scale, w_branch_pool, w_branch_attn, w_out,
              ffn2_norm, ffn2_w_gate, ffn2_w_up, ffn2_w_down, final_norm):
    B, S, _ = x.shape
    slopes = alibi_slopes(N_HEADS)
    h = x
    for l in range(DEPTH):
        h = h + 0.5 * swiglu(rms_norm(h, ffn1_norm[l]), ffn1_w_gate[l], ffn1_w_up[l], ffn1_w_down[l])

        u = rms_norm(h, mix_norm[l])
        proj = u @ w_in[l]
        o1 = POOL_WIDTH
        o2 = o1 + ATTN_WIDTH
        o3 = o2 + ATTN_WIDTH
        o4 = o3 + ATTN_WIDTH
        u_pool = proj[..., :o1]
        q = proj[..., o1:o2].reshape(B, S, N_HEADS, HEAD_DIM)
        k = proj[..., o2:o3].reshape(B, S, N_HEADS, HEAD_DIM)
        v = proj[..., o3:o4].reshape(B, S, N_HEADS, HEAD_DIM)
        gates = jax.nn.sigmoid(proj[..., o4:].astype(jnp.float32)).astype(h.dtype)
        gates = gates.reshape(B, S, N_BRANCHES, D_MODEL)

        y_pool = multiscale_pool(u_pool, pool_w[l], pool_scale[l]) @ w_branch_pool[l]
        y_attn = moba_attention(q, k, v, slopes) @ w_branch_attn[l]
        merged = gates[:, :, 0] * y_pool + gates[:, :, 1] * y_attn
        h = h + merged @ w_out[l]

        h = h + 0.5 * swiglu(rms_norm(h, ffn2_norm[l]), ffn2_w_gate[l], ffn2_w_up[l], ffn2_w_down[l])
    return rms_norm(h, final_norm)
```

```python
import functools

import jax
import jax.numpy as jnp
from jax import lax
from jax.experimental import pallas as pl
from jax.experimental.pallas import tpu as pltpu

EPS = 1e-6
D_MODEL = 1024
D_FF = 2816
POOL_WIDTH = 512
N_POOL_GROUPS = 4
POOL_GROUP_DIM = 128
POOL_WINDOWS = (2, 4, 8, 16)
ATTN_WIDTH = 512
N_HEADS = 8
HEAD_DIM = 64
N_PAIRS = N_HEADS // 2
MOBA_BLOCK = 256
MOBA_TOPK = 3
QKV_WIDTH = POOL_WIDTH + 3 * ATTN_WIDTH
NEG = -1e30

VMEM_LIMIT_BYTES = 58 * 1024 * 1024
FFN_TILE = 512
FFN_CHUNKS = ((0, 1024), (1024, 1024), (2048, 768))
INPROJ_TILE = 512
MAX_POOL_HALO = 16

BF16 = jnp.bfloat16
F32 = jnp.float32


def _rms_norm(x, g):
    ms = jnp.mean(x * x, axis=-1, keepdims=True)
    return x * lax.rsqrt(ms + EPS) * g


def _dot(a, b):
    return jnp.dot(a, b, preferred_element_type=F32)


def _dot_nt(a, b):
    return lax.dot_general(a, b, (((1,), (1,)), ((), ())),
                           preferred_element_type=F32)


def _ffn_kernel(*refs, final_norm):
    if final_norm:
        x_ref, g_ref, wg_ref, wu_ref, wd_ref, gf_ref, o_ref = refs
    else:
        x_ref, g_ref, wg_ref, wu_ref, wd_ref, o_ref = refs
    x = x_ref[...]
    xn = _rms_norm(x, g_ref[...]).astype(BF16)
    acc = None
    for start, size in FFN_CHUNKS:
        a = _dot(xn, wg_ref[:, start:start + size])
        b = _dot(xn, wu_ref[:, start:start + size])
        hmid = (a * jax.nn.sigmoid(a) * b).astype(BF16)
        d = _dot(hmid, wd_ref[start:start + size, :])
        acc = d if acc is None else acc + d
    h = x + 0.5 * acc
    if final_norm:
        h = _rms_norm(h, gf_ref[...])
    o_ref[...] = h


def _ffn(x2d, g, wg, wu, wd, gf=None):
    t, d = x2d.shape
    final_norm = gf is not None
    const = lambda i: (0, 0)
    in_specs = [
        pl.BlockSpec((FFN_TILE, d), lambda i: (i, 0)),
        pl.BlockSpec((1, d), const),
        pl.BlockSpec((d, D_FF), const, pipeline_mode=pl.Buffered(1)),
        pl.BlockSpec((d, D_FF), const, pipeline_mode=pl.Buffered(1)),
        pl.BlockSpec((D_FF, d), const, pipeline_mode=pl.Buffered(1)),
    ]
    args = [x2d, g, wg, wu, wd]
    if final_norm:
        in_specs.append(pl.BlockSpec((1, d), const))
        args.append(gf)
    return pl.pallas_call(
        functools.partial(_ffn_kernel, final_norm=final_norm),
        grid=(t // FFN_TILE,),
        in_specs=in_specs,
        out_specs=pl.BlockSpec((FFN_TILE, d), lambda i: (i, 0)),
        out_shape=jax.ShapeDtypeStruct((t, d), F32),
        compiler_params=pltpu.CompilerParams(
            dimension_semantics=("arbitrary",),
            vmem_limit_bytes=VMEM_LIMIT_BYTES),
        name="ffn_final" if final_norm else "ffn",
    )(*args)


def _inproj_kernel(h_ref, g_ref, w_ref, p_ref, q_ref, k_ref, v_ref, km_ref):
    u = _rms_norm(h_ref[...], g_ref[...]).astype(BF16)
    proj = _dot(u, w_ref[...])
    p_ref[...] = proj[:, :POOL_WIDTH]
    o = POOL_WIDTH
    q_ref[...] = (proj[:, o:o + ATTN_WIDTH] * (HEAD_DIM ** -0.5)).astype(BF16)
    k = proj[:, o + ATTN_WIDTH:o + 2 * ATTN_WIDTH]
    k_ref[...] = k.astype(BF16)
    v_ref[...] = proj[:, o + 2 * ATTN_WIDTH:o + 3 * ATTN_WIDTH].astype(BF16)
    for blk in range(INPROJ_TILE // MOBA_BLOCK):
        kb = k[blk * MOBA_BLOCK:(blk + 1) * MOBA_BLOCK, :]
        km_ref[0, blk:blk + 1, :] = jnp.mean(kb, axis=0, keepdims=True)


def _inproj(h2d, g, w_qkv):
    t, d = h2d.shape
    nblk = INPROJ_TILE // MOBA_BLOCK
    row = lambda i: (i, 0)
    const = lambda i: (0, 0)
    return pl.pallas_call(
        _inproj_kernel,
        grid=(t // INPROJ_TILE,),
        in_specs=[
            pl.BlockSpec((INPROJ_TILE, d), row),
            pl.BlockSpec((1, d), const),
            pl.BlockSpec((d, QKV_WIDTH), const, pipeline_mode=pl.Buffered(1)),
        ],
        out_specs=[
            pl.BlockSpec((INPROJ_TILE, POOL_WIDTH), row),
            pl.BlockSpec((INPROJ_TILE, ATTN_WIDTH), row),
            pl.BlockSpec((INPROJ_TILE, ATTN_WIDTH), row),
            pl.BlockSpec((INPROJ_TILE, ATTN_WIDTH), row),
            pl.BlockSpec((1, nblk, ATTN_WIDTH), lambda i: (i, 0, 0)),
        ],
        out_shape=[
            jax.ShapeDtypeStruct((t, POOL_WIDTH), F32),
            jax.ShapeDtypeStruct((t, ATTN_WIDTH), BF16),
            jax.ShapeDtypeStruct((t, ATTN_WIDTH), BF16),
            jax.ShapeDtypeStruct((t, ATTN_WIDTH), BF16),
            jax.ShapeDtypeStruct((t // INPROJ_TILE, nblk, ATTN_WIDTH), F32),
        ],
        compiler_params=pltpu.CompilerParams(
            dimension_semantics=("arbitrary",),
            vmem_limit_bytes=VMEM_LIMIT_BYTES),
        name="inproj",
    )(h2d, g, w_qkv)


def _mixer_kernel(h_ref, pc_ref, pp_ref, q_ref, k_ref, v_ref, km_ref,
                  gmix_ref, wgate_ref, poolw_ref, pscale_ref, wbp_ref, wba_ref,
                  wout_ref, o_ref,
                  pext_ref, qaug_ref, m_ref, acc_ref, attn_ref, ypool_ref):
    i = pl.program_id(1)
    nq = MOBA_BLOCK
    lane128 = lax.broadcasted_iota(jnp.int32, (nq, 128), 1)

    halo = pp_ref[0, nq - MAX_POOL_HALO:, :]
    pext_ref[:MAX_POOL_HALO, :] = jnp.where(i > 0, halo, jnp.zeros_like(halo))
    pext_ref[MAX_POOL_HALO:, :] = pc_ref[0]
    t_glob = i * nq + lax.broadcasted_iota(jnp.int32, (nq, POOL_GROUP_DIM), 0)
    for g, w in enumerate(POOL_WINDOWS):
        cols = slice(g * POOL_GROUP_DIM, (g + 1) * POOL_GROUP_DIM)
        xg = pext_ref[MAX_POOL_HALO:, cols]
        win = xg
        for s in range(1, w):
            win = win + pext_ref[MAX_POOL_HALO - s:MAX_POOL_HALO - s + nq, cols]
        count = jnp.minimum(t_glob + 1, w).astype(F32)
        mixed = (win / count - xg).astype(BF16)
        yg = _dot(mixed, poolw_ref[g]) * pscale_ref[:, cols]
        ypool_ref[:, cols] = yg.astype(BF16)
    y_pool = _dot(ypool_ref[...], wbp_ref[...])

    km = km_ref[0]
    head_of_lane = lax.broadcasted_iota(jnp.int32, (N_HEADS, ATTN_WIDTH), 1) // HEAD_DIM
    head_of_row = lax.broadcasted_iota(jnp.int32, (N_HEADS, ATTN_WIDTH), 0)
    head_mask = head_of_lane == head_of_row
    slabs = [jnp.where(head_mask, jnp.broadcast_to(km[n:n + 1, :], (N_HEADS, ATTN_WIDTH)), 0.0)
             for n in range(N_HEADS)]
    km_bd = jnp.concatenate(slabs, axis=0)
    km_hi = km_bd.astype(BF16)
    km_lo = (km_bd - km_hi.astype(F32)).astype(BF16)
    q = q_ref[0]
    gt = _dot_nt(jnp.concatenate([km_hi, km_lo], axis=0), q)
    gate = gt[:64, :] + gt[64:, :]
    g_n = [gate[n * 8:(n + 1) * 8, :] for n in range(8)]

    slope = jnp.exp2(-(lax.broadcasted_iota(jnp.int32, (N_HEADS, nq), 0) + 1).astype(F32))
    r_lane = lax.broadcasted_iota(jnp.int32, (N_HEADS, nq), 1).astype(F32)
    pre = []
    for n in range(8):
        cnt = jnp.zeros((N_HEADS, nq), F32)
        for n2 in range(8):
            if n2 == n:
                continue
            beats = (g_n[n2] >= g_n[n]) if n2 < n else (g_n[n2] > g_n[n])
            cnt = cnt + jnp.where(jnp.logical_and(beats, n2 < i), 1.0, 0.0)
        sel = jnp.logical_and(cnt < MOBA_TOPK, n < i)
        dist = (i - n).astype(F32) * float(MOBA_BLOCK)
        bias = jnp.where(sel, -slope * dist, NEG)
        pre.append(jnp.where(n == i, 0.0, bias))
    pre.append(-slope * r_lane)
    pre.append(slope)
    pre.append(jnp.zeros((128 - 80, nq), F32))
    q_ext_all = jnp.concatenate(pre, axis=0).T

    for p in range(N_PAIRS):
        qp = q[:, p * 128:(p + 1) * 128].astype(F32)
        for half in range(2):
            h = 2 * p + half
            in_head = (lane128 >= HEAD_DIM) if half else (lane128 < HEAD_DIM)
            qh = jnp.where(in_head, qp, 0.0).astype(BF16)
            ext_lane = jnp.logical_or(
                jnp.logical_and(lane128 < 64, lane128 % 8 == h),
                jnp.logical_or(lane128 == 64 + h, lane128 == 72 + h))
            qe = jnp.where(ext_lane, q_ext_all, 0.0).astype(BF16)
            rows = slice(half * nq, (half + 1) * nq)
            qaug_ref[p, rows, :128] = qh
            qaug_ref[p, rows, 128:] = qe

    m_ref[...] = jnp.full(m_ref.shape, NEG, F32)
    acc_ref[...] = jnp.zeros(acc_ref.shape, F32)

    key_row = lax.broadcasted_iota(jnp.int32, (nq, 128), 0)
    e_static = jnp.where(jnp.logical_and(lane128 >= 64, lane128 < 72), 1.0,
                         jnp.where(jnp.logical_and(lane128 >= 72, lane128 < 80),
                                   key_row.astype(F32), 0.0))
    ones_blk = jnp.ones((nq, 128), BF16)

    def visit(j, causal):
        start = pl.multiple_of(j * nq, nq)
        e_j = jnp.where(jnp.logical_and(lane128 < 64, lane128 // 8 == j), 1.0,
                        e_static).astype(BF16)
        for p in range(N_PAIRS):
            cols = slice(p * 128, (p + 1) * 128)
            k_aug = jnp.concatenate([k_ref[0, pl.ds(start, nq), cols], e_j], axis=1)
            s = _dot_nt(qaug_ref[p], k_aug)
            if causal:
                rr = lax.broadcasted_iota(jnp.int32, (2 * nq, nq), 0) % nq
                cc = lax.broadcasted_iota(jnp.int32, (2 * nq, nq), 1)
                s = jnp.where(cc <= rr, s, NEG)
            m_old = m_ref[p]
            m_new = jnp.maximum(m_old, jnp.max(s, axis=1, keepdims=True))
            alpha = jnp.exp(m_old - m_new)
            pexp = jnp.exp(s - jnp.concatenate([m_new, m_new], axis=1)).astype(BF16)
            v_ext = jnp.concatenate([v_ref[0, pl.ds(start, nq), cols], ones_blk], axis=1)
            r = _dot(pexp, v_ext)
            acc_ref[p] = acc_ref[p] * jnp.concatenate([alpha, alpha], axis=1) + r
            m_ref[p] = m_new

    def past_body(j, carry):
        visit(j, causal=False)
        return carry

    lax.fori_loop(0, i, past_body, 0)
    visit(i, causal=True)

    for p in range(N_PAIRS):
        acc = acc_ref[p]
        o_full = acc[:, :128] / acc[:, 128:]
        pair = jnp.where(lane128 < HEAD_DIM, o_full[:nq], o_full[nq:])
        attn_ref[:, p * 128:(p + 1) * 128] = pair.astype(BF16)
    y_attn = _dot(attn_ref[...], wba_ref[...])

    h1 = h_ref[0]
    u = _rms_norm(h1, gmix_ref[...]).astype(BF16)
    gates = jax.nn.sigmoid(_dot(u, wgate_ref[...]))
    merged = gates[:, :D_MODEL] * y_pool + gates[:, D_MODEL:] * y_attn
    o_ref[0] = h1 + _dot(merged.astype(BF16), wout_ref[...])


def _mixer(h1, p, q, k, v, kmean, g_mix, w_gate, pool_w, pool_scale, w_bp, w_ba, w_out):
    b, s, d = h1.shape
    nq = MOBA_BLOCK
    nt = s // nq
    tile = lambda bi, i: (bi, i, 0)
    prev = lambda bi, i: (bi, jnp.maximum(i - 1, 0), 0)
    seq = lambda bi, i: (bi, 0, 0)
    c2 = lambda bi, i: (0, 0)
    c3 = lambda bi, i: (0, 0, 0)
    one = pl.Buffered(1)
    return pl.pallas_call(
        _mixer_kernel,
        grid=(b, nt),
        in_specs=[
            pl.BlockSpec((1, nq, d), tile),
            pl.BlockSpec((1, nq, POOL_WIDTH), tile),
            pl.BlockSpec((1, nq, POOL_WIDTH), prev),
            pl.BlockSpec((1, nq, ATTN_WIDTH), tile),
            pl.BlockSpec((1, s, ATTN_WIDTH), seq),
            pl.BlockSpec((1, s, ATTN_WIDTH), seq),
            pl.BlockSpec((1, nt, ATTN_WIDTH), seq),
            pl.BlockSpec((1, d), c2),
            pl.BlockSpec((d, 2 * d), c2, pipeline_mode=one),
            pl.BlockSpec((N_POOL_GROUPS, POOL_GROUP_DIM, POOL_GROUP_DIM), c3, pipeline_mode=one),
            pl.BlockSpec((1, POOL_WIDTH), c2),
            pl.BlockSpec((POOL_WIDTH, d), c2, pipeline_mode=one),
            pl.BlockSpec((ATTN_WIDTH, d), c2, pipeline_mode=one),
            pl.BlockSpec((d, d), c2, pipeline_mode=one),
        ],
        out_specs=pl.BlockSpec((1, nq, d), tile),
        out_shape=jax.ShapeDtypeStruct((b, s, d), F32),
        scratch_shapes=[
            pltpu.VMEM((MAX_POOL_HALO + nq, POOL_WIDTH), F32),
            pltpu.VMEM((N_PAIRS, 2 * nq, 256), BF16),
            pltpu.VMEM((N_PAIRS, 2 * nq, 128), F32),
            pltpu.VMEM((N_PAIRS, 2 * nq, 256), F32),
            pltpu.VMEM((nq, ATTN_WIDTH), BF16),
            pltpu.VMEM((nq, POOL_WIDTH), BF16),
        ],
        compiler_params=pltpu.CompilerParams(
            dimension_semantics=("arbitrary", "arbitrary"),
            vmem_limit_bytes=VMEM_LIMIT_BYTES),
        name="mixer",
    )(h1, p, p, q, k, v, kmean, g_mix, w_gate, pool_w, pool_scale, w_bp, w_ba, w_out)


def kernel(x, ffn1_norm, ffn1_w_gate, ffn1_w_up, ffn1_w_down, mix_norm, w_in,
           pool_w, pool_scale, w_branch_pool, w_branch_attn, w_out,
           ffn2_norm, ffn2_w_gate, ffn2_w_up, ffn2_w_down, final_norm):
    b, s, d = x.shape
    assert d == D_MODEL and s % INPROJ_TILE == 0 and (b * s) % FFN_TILE == 0
    depth = ffn1_norm.shape[0]
    bf = lambda a: a.astype(BF16)
    h = x.reshape(b * s, d)
    for l in range(depth):
        last = l == depth - 1
        h = _ffn(h, ffn1_norm[l][None], bf(ffn1_w_gate[l]), bf(ffn1_w_up[l]),
                 bf(ffn1_w_down[l]))
        w_l = bf(w_in[l])
        p, q, k, v, kmean = _inproj(h, mix_norm[l][None], w_l[:, :QKV_WIDTH])
        shp = lambda a: a.reshape(b, s, a.shape[-1])
        h = _mixer(h.reshape(b, s, d), shp(p), shp(q), shp(k), shp(v),
                   kmean.reshape(b, s // MOBA_BLOCK, ATTN_WIDTH),
                   mix_norm[l][None], w_l[:, QKV_WIDTH:], bf(pool_w[l]),
                   pool_scale[l][None], bf(w_branch_pool[l]), bf(w_branch_attn[l]),
                   bf(w_out[l])).reshape(b * s, d)
        h = _ffn(h, ffn2_norm[l][None], bf(ffn2_w_gate[l]), bf(ffn2_w_up[l]),
                 bf(ffn2_w_down[l]), final_norm[None] if last else None)
    if depth == 0:
        raise ValueError("depth must be positive")
    return h.reshape(b, s, d)
```

```python
import functools

import jax
import jax.numpy as jnp
from jax import lax
from jax.experimental import pallas as pl
from jax.experimental.pallas import tpu as pltpu

EPS = 1e-6
D_MODEL = 1024
D_FF = 2816
POOL_WIDTH = 512
N_POOL_GROUPS = 4
POOL_GROUP_DIM = 128
POOL_WINDOWS = (2, 4, 8, 16)
ATTN_WIDTH = 512
N_HEADS = 8
HEAD_DIM = 64
N_PAIRS = N_HEADS // 2
MOBA_BLOCK = 256
MOBA_TOPK = 3
QKV_WIDTH = POOL_WIDTH + 3 * ATTN_WIDTH
NEG = -1e30

VMEM_LIMIT_BYTES = 58 * 1024 * 1024
FFN_TILE = 512
FFN_CHUNKS = ((0, 1024), (1024, 1024), (2048, 768))
INPROJ_TILE = 512
MAX_POOL_HALO = 16

BF16 = jnp.bfloat16
F32 = jnp.float32


def _rms_norm(x, g):
    ms = jnp.mean(x * x, axis=-1, keepdims=True)
    return x * lax.rsqrt(ms + EPS) * g


def _dot(a, b):
    return jnp.dot(a, b, preferred_element_type=F32)


def _dot_nt(a, b):
    return lax.dot_general(a, b, (((1,), (1,)), ((), ())),
                           preferred_element_type=F32)


def _ffn_kernel(*refs, final_norm):
    if final_norm:
        x_ref, g_ref, wg_ref, wu_ref, wd_ref, gf_ref, o_ref = refs
    else:
        x_ref, g_ref, wg_ref, wu_ref, wd_ref, o_ref = refs
    x = x_ref[...]
    xn = _rms_norm(x, g_ref[...]).astype(BF16)
    acc = None
    for start, size in FFN_CHUNKS:
        a = _dot(xn, wg_ref[:, start:start + size])
        b = _dot(xn, wu_ref[:, start:start + size])
        hmid = (a * jax.nn.sigmoid(a) * b).astype(BF16)
        d = _dot(hmid, wd_ref[start:start + size, :])
        acc = d if acc is None else acc + d
    h = x + 0.5 * acc
    if final_norm:
        h = _rms_norm(h, gf_ref[...])
    o_ref[...] = h


def _ffn(x2d, g, wg, wu, wd, gf=None):
    t, d = x2d.shape
    final_norm = gf is not None
    const = lambda i: (0, 0)
    in_specs = [
        pl.BlockSpec((FFN_TILE, d), lambda i: (i, 0)),
        pl.BlockSpec((1, d), const),
        pl.BlockSpec((d, D_FF), const, pipeline_mode=pl.Buffered(1)),
        pl.BlockSpec((d, D_FF), const, pipeline_mode=pl.Buffered(1)),
        pl.BlockSpec((D_FF, d), const, pipeline_mode=pl.Buffered(1)),
    ]
    args = [x2d, g, wg, wu, wd]
    if final_norm:
        in_specs.append(pl.BlockSpec((1, d), const))
        args.append(gf)
    return pl.pallas_call(
        functools.partial(_ffn_kernel, final_norm=final_norm),
        grid=(t // FFN_TILE,),
        in_specs=in_specs,
        out_specs=pl.BlockSpec((FFN_TILE, d), lambda i: (i, 0)),
        out_shape=jax.ShapeDtypeStruct((t, d), F32),
        compiler_params=pltpu.CompilerParams(
            dimension_semantics=("arbitrary",),
            vmem_limit_bytes=VMEM_LIMIT_BYTES),
        name="ffn_final" if final_norm else "ffn",
    )(*args)


def _inproj_kernel(h_ref, g_ref, w_ref, p_ref, q_ref, k_ref, v_ref, km_ref):
    u = _rms_norm(h_ref[...], g_ref[...]).astype(BF16)
    proj = _dot(u, w_ref[...])
    p_ref[...] = proj[:, :POOL_WIDTH]
    o = POOL_WIDTH
    q_ref[...] = (proj[:, o:o + ATTN_WIDTH] * (HEAD_DIM ** -0.5)).astype(BF16)
    k = proj[:, o + ATTN_WIDTH:o + 2 * ATTN_WIDTH]
    k_ref[...] = k.astype(BF16)
    v_ref[...] = proj[:, o + 2 * ATTN_WIDTH:o + 3 * ATTN_WIDTH].astype(BF16)
    for blk in range(INPROJ_TILE // MOBA_BLOCK):
        kb = k[blk * MOBA_BLOCK:(blk + 1) * MOBA_BLOCK, :]
        km_ref[0, blk:blk + 1, :] = jnp.mean(kb, axis=0, keepdims=True)


def _inproj(h2d, g, w_qkv):
    t, d = h2d.shape
    nblk = INPROJ_TILE // MOBA_BLOCK
    row = lambda i: (i, 0)
    const = lambda i: (0, 0)
    return pl.pallas_call(
        _inproj_kernel,
        grid=(t // INPROJ_TILE,),
        in_specs=[
            pl.BlockSpec((INPROJ_TILE, d), row),
            pl.BlockSpec((1, d), const),
            pl.BlockSpec((d, QKV_WIDTH), const, pipeline_mode=pl.Buffered(1)),
        ],
        out_specs=[
            pl.BlockSpec((INPROJ_TILE, POOL_WIDTH), row),
            pl.BlockSpec((INPROJ_TILE, ATTN_WIDTH), row),
            pl.BlockSpec((INPROJ_TILE, ATTN_WIDTH), row),
            pl.BlockSpec((INPROJ_TILE, ATTN_WIDTH), row),
            pl.BlockSpec((1, nblk, ATTN_WIDTH), lambda i: (i, 0, 0)),
        ],
        out_shape=[
            jax.ShapeDtypeStruct((t, POOL_WIDTH), F32),
            jax.ShapeDtypeStruct((t, ATTN_WIDTH), BF16),
            jax.ShapeDtypeStruct((t, ATTN_WIDTH), BF16),
            jax.ShapeDtypeStruct((t, ATTN_WIDTH), BF16),
            jax.ShapeDtypeStruct((t // INPROJ_TILE, nblk, ATTN_WIDTH), F32),
        ],
        compiler_params=pltpu.CompilerParams(
            dimension_semantics=("arbitrary",),
            vmem_limit_bytes=VMEM_LIMIT_BYTES),
        name="inproj",
    )(h2d, g, w_qkv)


def _mixer_kernel(h_ref, pc_ref, pp_ref, q_ref, k_ref, v_ref, km_ref,
                  gmix_ref, wgate_ref, poolw_ref, pscale_ref, wbp_ref, wba_ref,
                  wout_ref, o_ref,
                  pext_ref, qaug_ref, mx_ref, s_ref, acc_ref, attn_ref, ypool_ref):
    i = pl.program_id(1)
    nq = MOBA_BLOCK
    lane128 = lax.broadcasted_iota(jnp.int32, (nq, 128), 1)

    halo = pp_ref[0, nq - MAX_POOL_HALO:, :]
    pext_ref[:MAX_POOL_HALO, :] = jnp.where(i > 0, halo, jnp.zeros_like(halo))
    pext_ref[MAX_POOL_HALO:, :] = pc_ref[0]
    t_glob = i * nq + lax.broadcasted_iota(jnp.int32, (nq, POOL_GROUP_DIM), 0)
    for g, w in enumerate(POOL_WINDOWS):
        cols = slice(g * POOL_GROUP_DIM, (g + 1) * POOL_GROUP_DIM)
        xg = pext_ref[MAX_POOL_HALO:, cols]
        win = xg
        for s in range(1, w):
            win = win + pext_ref[MAX_POOL_HALO - s:MAX_POOL_HALO - s + nq, cols]
        count = jnp.minimum(t_glob + 1, w).astype(F32)
        mixed = (win / count - xg).astype(BF16)
        yg = _dot(mixed, poolw_ref[g]) * pscale_ref[:, cols]
        ypool_ref[:, cols] = yg.astype(BF16)
    y_pool = _dot(ypool_ref[...], wbp_ref[...])

    km = km_ref[0]
    head_of_lane = lax.broadcasted_iota(jnp.int32, (N_HEADS, ATTN_WIDTH), 1) // HEAD_DIM
    head_of_row = lax.broadcasted_iota(jnp.int32, (N_HEADS, ATTN_WIDTH), 0)
    head_mask = head_of_lane == head_of_row
    slabs = [jnp.where(head_mask, jnp.broadcast_to(km[n:n + 1, :], (N_HEADS, ATTN_WIDTH)), 0.0)
             for n in range(N_HEADS)]
    km_bd = jnp.concatenate(slabs, axis=0)
    km_hi = km_bd.astype(BF16)
    km_lo = (km_bd - km_hi.astype(F32)).astype(BF16)
    q = q_ref[0]
    gt = _dot_nt(jnp.concatenate([km_hi, km_lo], axis=0), q)
    gate = gt[:64, :] + gt[64:, :]
    g_n = [gate[n * 8:(n + 1) * 8, :] for n in range(8)]

    slope = jnp.exp2(-(lax.broadcasted_iota(jnp.int32, (N_HEADS, nq), 0) + 1).astype(F32))
    r_lane = lax.broadcasted_iota(jnp.int32, (N_HEADS, nq), 1).astype(F32)
    pre = []
    for n in range(8):
        cnt = jnp.zeros((N_HEADS, nq), F32)
        for n2 in range(8):
            if n2 == n:
                continue
            beats = (g_n[n2] >= g_n[n]) if n2 < n else (g_n[n2] > g_n[n])
            cnt = cnt + jnp.where(jnp.logical_and(beats, n2 < i), 1.0, 0.0)
        sel = jnp.logical_and(cnt < MOBA_TOPK, n < i)
        dist = (i - n).astype(F32) * float(MOBA_BLOCK)
        bias = jnp.where(sel, -slope * dist, NEG)
        pre.append(jnp.where(n == i, 0.0, bias))
    pre.append(-slope * r_lane)
    pre.append(slope)
    pre.append(jnp.zeros((128 - 80, nq), F32))
    q_ext_all = jnp.concatenate(pre, axis=0).T

    for p in range(N_PAIRS):
        qp = q[:, p * 128:(p + 1) * 128].astype(F32)
        for half in range(2):
            h = 2 * p + half
            in_head = (lane128 >= HEAD_DIM) if half else (lane128 < HEAD_DIM)
            qh = jnp.where(in_head, qp, 0.0).astype(BF16)
            ext_lane = jnp.logical_or(
                jnp.logical_and(lane128 < 64, lane128 % 8 == h),
                jnp.logical_or(lane128 == 64 + h, lane128 == 72 + h))
            qe = jnp.where(ext_lane, q_ext_all, 0.0).astype(BF16)
            rows = slice(half * nq, (half + 1) * nq)
            qaug_ref[p, rows, :128] = qh
            qaug_ref[p, rows, 128:] = qe

    mx_ref[...] = jnp.full(mx_ref.shape, NEG, F32)
    acc_ref[...] = jnp.zeros(acc_ref.shape, F32)

    key_row = lax.broadcasted_iota(jnp.int32, (nq, 128), 0)
    e_static = jnp.where(jnp.logical_and(lane128 >= 64, lane128 < 72), 1.0,
                         jnp.where(jnp.logical_and(lane128 >= 72, lane128 < 80),
                                   key_row.astype(F32), 0.0))
    ones_blk = jnp.ones((nq, 128), BF16)

    def score_block(j, causal):
        start = pl.multiple_of(j * nq, nq)
        e_j = jnp.where(jnp.logical_and(lane128 < 64, lane128 // 8 == j), 1.0,
                        e_static).astype(BF16)
        for p in range(N_PAIRS):
            cols = slice(p * 128, (p + 1) * 128)
            k_aug = jnp.concatenate([k_ref[0, pl.ds(start, nq), cols], e_j], axis=1)
            s = _dot_nt(qaug_ref[p], k_aug)
            if causal:
                rr = lax.broadcasted_iota(jnp.int32, (2 * nq, nq), 0) % nq
                cc = lax.broadcasted_iota(jnp.int32, (2 * nq, nq), 1)
                s = jnp.where(cc <= rr, s, NEG)
            s_ref[p, :, pl.ds(start, nq)] = s
            mx_ref[p] = jnp.maximum(mx_ref[p], jnp.maximum(s[:, :128], s[:, 128:]))

    def score_body(j, carry):
        score_block(j, causal=False)
        return carry

    lax.fori_loop(0, i, score_body, 0)
    score_block(i, causal=True)

    for p in range(N_PAIRS):
        m = jnp.max(mx_ref[p], axis=1, keepdims=True)
        mx_ref[p] = jnp.broadcast_to(m, (2 * nq, 128))

    def value_body(j, carry):
        start = pl.multiple_of(j * nq, nq)
        for p in range(N_PAIRS):
            cols = slice(p * 128, (p + 1) * 128)
            m = mx_ref[p]
            s = s_ref[p, :, pl.ds(start, nq)]
            pexp = jnp.exp(s - jnp.concatenate([m, m], axis=1)).astype(BF16)
            v_ext = jnp.concatenate([v_ref[0, pl.ds(start, nq), cols], ones_blk], axis=1)
            acc_ref[p] += _dot(pexp, v_ext)
        return carry

    lax.fori_loop(0, i + 1, value_body, 0)

    for p in range(N_PAIRS):
        acc = acc_ref[p]
        o_full = acc[:, :128] / acc[:, 128:]
        pair = jnp.where(lane128 < HEAD_DIM, o_full[:nq], o_full[nq:])
        attn_ref[:, p * 128:(p + 1) * 128] = pair.astype(BF16)
    y_attn = _dot(attn_ref[...], wba_ref[...])

    h1 = h_ref[0]
    u = _rms_norm(h1, gmix_ref[...]).astype(BF16)
    gates = 0.5 * jnp.tanh(0.5 * _dot(u, wgate_ref[...])) + 0.5
    merged = gates[:, :D_MODEL] * y_pool + gates[:, D_MODEL:] * y_attn
    o_ref[0] = h1 + _dot(merged.astype(BF16), wout_ref[...])


def _mixer(h1, p, q, k, v, kmean, g_mix, w_gate, pool_w, pool_scale, w_bp, w_ba, w_out):
    b, s, d = h1.shape
    nq = MOBA_BLOCK
    nt = s // nq
    tile = lambda bi, i: (bi, i, 0)
    prev = lambda bi, i: (bi, jnp.maximum(i - 1, 0), 0)
    seq = lambda bi, i: (bi, 0, 0)
    c2 = lambda bi, i: (0, 0)
    c3 = lambda bi, i: (0, 0, 0)
    one = pl.Buffered(1)
    return pl.pallas_call(
        _mixer_kernel,
        grid=(b, nt),
        in_specs=[
            pl.BlockSpec((1, nq, d), tile),
            pl.BlockSpec((1, nq, POOL_WIDTH), tile),
            pl.BlockSpec((1, nq, POOL_WIDTH), prev),
            pl.BlockSpec((1, nq, ATTN_WIDTH), tile),
            pl.BlockSpec((1, s, ATTN_WIDTH), seq),
            pl.BlockSpec((1, s, ATTN_WIDTH), seq),
            pl.BlockSpec((1, nt, ATTN_WIDTH), seq),
            pl.BlockSpec((1, d), c2),
            pl.BlockSpec((d, 2 * d), c2, pipeline_mode=one),
            pl.BlockSpec((N_POOL_GROUPS, POOL_GROUP_DIM, POOL_GROUP_DIM), c3, pipeline_mode=one),
            pl.BlockSpec((1, POOL_WIDTH), c2),
            pl.BlockSpec((POOL_WIDTH, d), c2, pipeline_mode=one),
            pl.BlockSpec((ATTN_WIDTH, d), c2, pipeline_mode=one),
            pl.BlockSpec((d, d), c2, pipeline_mode=one),
        ],
        out_specs=pl.BlockSpec((1, nq, d), tile),
        out_shape=jax.ShapeDtypeStruct((b, s, d), F32),
        scratch_shapes=[
            pltpu.VMEM((MAX_POOL_HALO + nq, POOL_WIDTH), F32),
            pltpu.VMEM((N_PAIRS, 2 * nq, 256), BF16),
            pltpu.VMEM((N_PAIRS, 2 * nq, 128), F32),
            pltpu.VMEM((N_PAIRS, 2 * nq, s), F32),
            pltpu.VMEM((N_PAIRS, 2 * nq, 256), F32),
            pltpu.VMEM((nq, ATTN_WIDTH), BF16),
            pltpu.VMEM((nq, POOL_WIDTH), BF16),
        ],
        compiler_params=pltpu.CompilerParams(
            dimension_semantics=("arbitrary", "arbitrary"),
            vmem_limit_bytes=VMEM_LIMIT_BYTES),
        name="mixer",
    )(h1, p, p, q, k, v, kmean, g_mix, w_gate, pool_w, pool_scale, w_bp, w_ba, w_out)


def kernel(x, ffn1_norm, ffn1_w_gate, ffn1_w_up, ffn1_w_down, mix_norm, w_in,
           pool_w, pool_scale, w_branch_pool, w_branch_attn, w_out,
           ffn2_norm, ffn2_w_gate, ffn2_w_up, ffn2_w_down, final_norm):
    b, s, d = x.shape
    assert d == D_MODEL and s % INPROJ_TILE == 0 and (b * s) % FFN_TILE == 0
    depth = ffn1_norm.shape[0]
    bf = lambda a: a.astype(BF16)
    h = x.reshape(b * s, d)
    for l in range(depth):
        last = l == depth - 1
        h = _ffn(h, ffn1_norm[l][None], bf(ffn1_w_gate[l]), bf(ffn1_w_up[l]),
                 bf(ffn1_w_down[l]))
        w_l = bf(w_in[l])
        p, q, k, v, kmean = _inproj(h, mix_norm[l][None], w_l[:, :QKV_WIDTH])
        shp = lambda a: a.reshape(b, s, a.shape[-1])
        h = _mixer(h.reshape(b, s, d), shp(p), shp(q), shp(k), shp(v),
                   kmean.reshape(b, s // MOBA_BLOCK, ATTN_WIDTH),
                   mix_norm[l][None], w_l[:, QKV_WIDTH:], bf(pool_w[l]),
                   pool_scale[l][None], bf(w_branch_pool[l]), bf(w_branch_attn[l]),
                   bf(w_out[l])).reshape(b * s, d)
        h = _ffn(h, ffn2_norm[l][None], bf(ffn2_w_gate[l]), bf(ffn2_w_up[l]),
                 bf(ffn2_w_down[l]), final_norm[None] if last else None)
    if depth == 0:
        raise ValueError("depth must be positive")
    return h.reshape(b, s, d)
```

```python
import functools

import jax
import jax.numpy as jnp
from jax import lax
from jax.experimental import pallas as pl
from jax.experimental.pallas import tpu as pltpu

EPS = 1e-6
D_MODEL = 1024
D_FF = 2816
POOL_WIDTH = 512
N_POOL_GROUPS = 4
POOL_GROUP_DIM = 128
POOL_WINDOWS = (2, 4, 8, 16)
ATTN_WIDTH = 512
N_HEADS = 8
HEAD_DIM = 64
N_PAIRS = N_HEADS // 2
MOBA_BLOCK = 256
MOBA_TOPK = 3
N_BLOCKS = 8
N_VISITS = N_BLOCKS + 1
QKV_WIDTH = POOL_WIDTH + 3 * ATTN_WIDTH
NEG = -1e30

VMEM_LIMIT_BYTES = 58 * 1024 * 1024
FFN_TILE = 512
FFN_CHUNKS = ((0, 1024), (1024, 1024), (2048, 768))
INPROJ_TILE = 512
MAX_POOL_HALO = 16

BF16 = jnp.bfloat16
F32 = jnp.float32


def _rms_norm(x, g):
    ms = jnp.mean(x * x, axis=-1, keepdims=True)
    return x * lax.rsqrt(ms + EPS) * g


def _dot(a, b):
    return jnp.dot(a, b, preferred_element_type=F32)


def _dot_nt(a, b):
    return lax.dot_general(a, b, (((1,), (1,)), ((), ())),
                           preferred_element_type=F32)


def _ffn_body(x, g_ref, wg_ref, wu_ref, wd_ref, gf_ref):
    xn = _rms_norm(x, g_ref[...]).astype(BF16)
    acc = None
    for start, size in FFN_CHUNKS:
        a = _dot(xn, wg_ref[:, start:start + size])
        b = _dot(xn, wu_ref[:, start:start + size])
        hmid = (a * jax.nn.sigmoid(a) * b).astype(BF16)
        d = _dot(hmid, wd_ref[start:start + size, :])
        acc = d if acc is None else acc + d
    h = x + 0.5 * acc
    if gf_ref is not None:
        h = _rms_norm(h, gf_ref[...])
    return h


def _ffn_kernel(x_ref, g_ref, wg_ref, wu_ref, wd_ref, o_ref):
    o_ref[...] = _ffn_body(x_ref[...], g_ref, wg_ref, wu_ref, wd_ref, None)


def _ffn_weight_specs(d, const):
    one = pl.Buffered(1)
    return [
        pl.BlockSpec((1, d), const),
        pl.BlockSpec((d, D_FF), const, pipeline_mode=one),
        pl.BlockSpec((d, D_FF), const, pipeline_mode=one),
        pl.BlockSpec((D_FF, d), const, pipeline_mode=one),
    ]


def _ffn(x2d, g, wg, wu, wd):
    t, d = x2d.shape
    return pl.pallas_call(
        _ffn_kernel,
        grid=(t // FFN_TILE,),
        in_specs=[pl.BlockSpec((FFN_TILE, d), lambda i: (i, 0))]
        + _ffn_weight_specs(d, lambda i: (0, 0)),
        out_specs=pl.BlockSpec((FFN_TILE, d), lambda i: (i, 0)),
        out_shape=jax.ShapeDtypeStruct((t, d), F32),
        compiler_params=pltpu.CompilerParams(
            dimension_semantics=("arbitrary",),
            vmem_limit_bytes=VMEM_LIMIT_BYTES),
        name="ffn",
    )(x2d, g, wg, wu, wd)


def _ffn_paired_kernel(lo_ref, hi_ref, g_ref, wg_ref, wu_ref, wd_ref, gf_ref, o_ref,
                       *, final_norm):
    m = pl.program_id(1)
    half = FFN_TILE // 2
    hi = hi_ref[0]
    x = jnp.where(m < 2, lo_ref[0], jnp.concatenate([hi[half:], hi[:half]], axis=0))
    o_ref[0] = _ffn_body(x, g_ref, wg_ref, wu_ref, wd_ref, gf_ref if final_norm else None)


def _ffn_paired(h_lo, h_hi, g, wg, wu, wd, gf, final_norm):
    b, half_s, d = h_lo.shape
    steps = 2 * half_s // FFN_TILE
    const = lambda bi, m: (0, 0)
    return pl.pallas_call(
        functools.partial(_ffn_paired_kernel, final_norm=final_norm),
        grid=(b, steps),
        in_specs=[
            pl.BlockSpec((1, FFN_TILE, d), lambda bi, m: (bi, jnp.minimum(m, 1), 0)),
            pl.BlockSpec((1, FFN_TILE, d), lambda bi, m: (bi, jnp.clip(3 - m, 0, 1), 0)),
        ] + _ffn_weight_specs(d, const) + [pl.BlockSpec((1, d), const)],
        out_specs=pl.BlockSpec((1, FFN_TILE, d), lambda bi, m: (bi, m, 0)),
        out_shape=jax.ShapeDtypeStruct((b, 2 * half_s, d), F32),
        compiler_params=pltpu.CompilerParams(
            dimension_semantics=("arbitrary", "arbitrary"),
            vmem_limit_bytes=VMEM_LIMIT_BYTES),
        name="ffn_final" if final_norm else "ffn_paired",
    )(h_lo, h_hi, g, wg, wu, wd, gf)


def _inproj_kernel(h_ref, g_ref, w_ref, p_ref, q_ref, k_ref, v_ref, km_ref):
    u = _rms_norm(h_ref[...], g_ref[...]).astype(BF16)
    proj = _dot(u, w_ref[...])
    p_ref[...] = proj[:, :POOL_WIDTH]
    o = POOL_WIDTH
    q_ref[...] = (proj[:, o:o + ATTN_WIDTH] * (HEAD_DIM ** -0.5)).astype(BF16)
    k = proj[:, o + ATTN_WIDTH:o + 2 * ATTN_WIDTH]
    k_ref[...] = k.astype(BF16)
    v_ref[...] = proj[:, o + 2 * ATTN_WIDTH:o + 3 * ATTN_WIDTH].astype(BF16)
    for blk in range(INPROJ_TILE // MOBA_BLOCK):
        kb = k[blk * MOBA_BLOCK:(blk + 1) * MOBA_BLOCK, :]
        km_ref[0, blk:blk + 1, :] = jnp.mean(kb, axis=0, keepdims=True)


def _inproj(h2d, g, w_qkv):
    t, d = h2d.shape
    nblk = INPROJ_TILE // MOBA_BLOCK
    row = lambda i: (i, 0)
    const = lambda i: (0, 0)
    return pl.pallas_call(
        _inproj_kernel,
        grid=(t // INPROJ_TILE,),
        in_specs=[
            pl.BlockSpec((INPROJ_TILE, d), row),
            pl.BlockSpec((1, d), const),
            pl.BlockSpec((d, QKV_WIDTH), const, pipeline_mode=pl.Buffered(1)),
        ],
        out_specs=[
            pl.BlockSpec((INPROJ_TILE, POOL_WIDTH), row),
            pl.BlockSpec((INPROJ_TILE, ATTN_WIDTH), row),
            pl.BlockSpec((INPROJ_TILE, ATTN_WIDTH), row),
            pl.BlockSpec((INPROJ_TILE, ATTN_WIDTH), row),
            pl.BlockSpec((1, nblk, ATTN_WIDTH), lambda i: (i, 0, 0)),
        ],
        out_shape=[
            jax.ShapeDtypeStruct((t, POOL_WIDTH), F32),
            jax.ShapeDtypeStruct((t, ATTN_WIDTH), BF16),
            jax.ShapeDtypeStruct((t, ATTN_WIDTH), BF16),
            jax.ShapeDtypeStruct((t, ATTN_WIDTH), BF16),
            jax.ShapeDtypeStruct((t // INPROJ_TILE, nblk, ATTN_WIDTH), F32),
        ],
        compiler_params=pltpu.CompilerParams(
            dimension_semantics=("arbitrary",),
            vmem_limit_bytes=VMEM_LIMIT_BYTES),
        name="inproj",
    )(h2d, g, w_qkv)


def _pool_branch(i, p_ref, halo_ref, poolw_ref, pscale_ref, pext_ref, ypool_ref, row0):
    nq = MOBA_BLOCK
    halo = halo_ref[0]
    pext_ref[:MAX_POOL_HALO, :] = jnp.where(i > 0, halo, jnp.zeros_like(halo))
    pext_ref[MAX_POOL_HALO:, :] = p_ref[0]
    t_glob = i * nq + lax.broadcasted_iota(jnp.int32, (nq, POOL_GROUP_DIM), 0)
    for g, w in enumerate(POOL_WINDOWS):
        cols = slice(g * POOL_GROUP_DIM, (g + 1) * POOL_GROUP_DIM)
        xg = pext_ref[MAX_POOL_HALO:, cols]
        win = xg
        for s in range(1, w):
            win = win + pext_ref[MAX_POOL_HALO - s:MAX_POOL_HALO - s + nq, cols]
        count = jnp.minimum(t_glob + 1, w).astype(F32)
        mixed = (win / count - xg).astype(BF16)
        yg = _dot(mixed, poolw_ref[g]) * pscale_ref[:, cols]
        ypool_ref[row0:row0 + nq, cols] = yg.astype(BF16)


def _augmented_queries(i, q, km_stack, qaug_ref, tile):
    nq = MOBA_BLOCK
    lane128 = lax.broadcasted_iota(jnp.int32, (nq, 128), 1)
    gt = _dot_nt(km_stack, q)
    gate = gt[:64, :] + gt[64:, :]
    g_n = [gate[n * 8:(n + 1) * 8, :] for n in range(N_BLOCKS)]
    slope = jnp.exp2(-(lax.broadcasted_iota(jnp.int32, (N_HEADS, nq), 0) + 1).astype(F32))
    r_lane = lax.broadcasted_iota(jnp.int32, (N_HEADS, nq), 1).astype(F32)
    pre = []
    for n in range(N_BLOCKS):
        cnt = jnp.zeros((N_HEADS, nq), F32)
        for n2 in range(N_BLOCKS):
            if n2 == n:
                continue
            beats = (g_n[n2] >= g_n[n]) if n2 < n else (g_n[n2] > g_n[n])
            cnt = cnt + jnp.where(jnp.logical_and(beats, n2 < i), 1.0, 0.0)
        sel = jnp.logical_and(cnt < MOBA_TOPK, n < i)
        dist = (i - n).astype(F32) * float(MOBA_BLOCK)
        bias = jnp.where(sel, -slope * dist, NEG)
        pre.append(jnp.where(n == i, 0.0, bias))
    pre.append(-slope * r_lane)
    pre.append(slope)
    pre.append(jnp.zeros((128 - 80, nq), F32))
    q_ext_all = jnp.concatenate(pre, axis=0).T
    for p in range(N_PAIRS):
        qp = q[:, p * 128:(p + 1) * 128].astype(F32)
        for half in range(2):
            h = 2 * p + half
            in_head = (lane128 >= HEAD_DIM) if half else (lane128 < HEAD_DIM)
            qh = jnp.where(in_head, qp, 0.0).astype(BF16)
            ext_lane = jnp.logical_or(
                jnp.logical_and(lane128 < 64, lane128 % 8 == h),
                jnp.logical_or(lane128 == 64 + h, lane128 == 72 + h))
            qe = jnp.where(ext_lane, q_ext_all, 0.0).astype(BF16)
            rows = slice(half * nq, (half + 1) * nq)
            qaug_ref[tile, p, rows, :128] = qh
            qaug_ref[tile, p, rows, 128:] = qe


def _mixer_kernel(hlo_ref, hhi_ref, plo_ref, phi_ref, halo_lo_ref, halo_hi_ref,
                  qlo_ref, qhi_ref, k_ref, v_ref, km_ref,
                  gmix_ref, wgate_ref, poolw_ref, pscale_ref, wbp_ref, wba_ref, wout_ref,
                  olo_ref, ohi_ref,
                  pext_ref, qaug_ref, mxv_ref, m_ref, s_ref, acc_ref, attn_ref, ypool_ref):
    t = pl.program_id(1)
    nq = MOBA_BLOCK
    i_lo, i_hi = t, N_BLOCKS - 1 - t
    lane128 = lax.broadcasted_iota(jnp.int32, (nq, 128), 1)

    _pool_branch(i_lo, plo_ref, halo_lo_ref, poolw_ref, pscale_ref, pext_ref, ypool_ref, 0)
    _pool_branch(i_hi, phi_ref, halo_hi_ref, poolw_ref, pscale_ref, pext_ref, ypool_ref, nq)
    y_pool = _dot(ypool_ref[...], wbp_ref[...])

    km = km_ref[0]
    head_of_lane = lax.broadcasted_iota(jnp.int32, (N_HEADS, ATTN_WIDTH), 1) // HEAD_DIM
    head_of_row = lax.broadcasted_iota(jnp.int32, (N_HEADS, ATTN_WIDTH), 0)
    head_mask = head_of_lane == head_of_row
    slabs = [jnp.where(head_mask, jnp.broadcast_to(km[n:n + 1, :], (N_HEADS, ATTN_WIDTH)), 0.0)
             for n in range(N_BLOCKS)]
    km_bd = jnp.concatenate(slabs, axis=0)
    km_hi = km_bd.astype(BF16)
    km_lo = (km_bd - km_hi.astype(F32)).astype(BF16)
    km_stack = jnp.concatenate([km_hi, km_lo], axis=0)
    _augmented_queries(i_lo, qlo_ref[0], km_stack, qaug_ref, 0)
    _augmented_queries(i_hi, qhi_ref[0], km_stack, qaug_ref, 1)

    key_row = lax.broadcasted_iota(jnp.int32, (nq, 128), 0)
    e_static = jnp.where(jnp.logical_and(lane128 >= 64, lane128 < 72), 1.0,
                         jnp.where(jnp.logical_and(lane128 >= 72, lane128 < 80),
                                   key_row.astype(F32), 0.0))
    ones_blk = jnp.ones((nq, 128), BF16)
    rr = lax.broadcasted_iota(jnp.int32, (2 * nq, nq), 0) % nq
    cc = lax.broadcasted_iota(jnp.int32, (2 * nq, nq), 1)
    causal_ok = cc <= rr

    n_dyn = N_BLOCKS // 2
    visits = []
    for v in range(N_VISITS):
        is_lo = v <= t
        tile = jnp.where(is_lo, 0, 1) if v < n_dyn else 1
        j = jnp.where(is_lo, v, v - t - 1)
        start = pl.multiple_of(j * nq, nq)
        e_j = jnp.where(jnp.logical_and(lane128 < 64, lane128 // 8 == j), 1.0,
                        e_static).astype(BF16)
        visits.append((is_lo, tile, start, e_j))

    def score_visit(p, v):
        _, tile, start, e_j = visits[v]
        cols = slice(p * 128, (p + 1) * 128)
        k_aug = jnp.concatenate([k_ref[0, pl.ds(start, nq), cols], e_j], axis=1)
        s = _dot_nt(qaug_ref[tile, p], k_aug)
        if v == N_VISITS - 1:
            s = jnp.where(causal_ok, s, NEG)
        elif v < n_dyn:
            s = jnp.where(jnp.logical_or(causal_ok, v != t), s, NEG)
        s_ref[p % 2, v] = s
        mxv_ref[p % 2, v] = jnp.maximum(s[:, :128], s[:, 128:])

    def row_max(p):
        buf = p % 2
        m_hi = mxv_ref[buf, n_dyn]
        for v in range(n_dyn + 1, N_VISITS):
            m_hi = jnp.maximum(m_hi, mxv_ref[buf, v])
        m_lo = jnp.full((2 * nq, 128), NEG, F32)
        for v in range(n_dyn):
            mv = mxv_ref[buf, v]
            m_lo = jnp.maximum(m_lo, jnp.where(visits[v][0], mv, NEG))
            m_hi = jnp.maximum(m_hi, jnp.where(visits[v][0], NEG, mv))
        m_ref[buf, 0] = jnp.broadcast_to(jnp.max(m_lo, axis=1, keepdims=True), (2 * nq, 128))
        m_ref[buf, 1] = jnp.broadcast_to(jnp.max(m_hi, axis=1, keepdims=True), (2 * nq, 128))

    def value_visit(p, v):
        buf = p % 2
        _, tile, start, _ = visits[v]
        cols = slice(p * 128, (p + 1) * 128)
        m = m_ref[buf, tile]
        pexp = jnp.exp(s_ref[buf, v] - jnp.concatenate([m, m], axis=1)).astype(BF16)
        v_ext = jnp.concatenate([v_ref[0, pl.ds(start, nq), cols], ones_blk], axis=1)
        r = _dot(pexp, v_ext)
        run = r if v == 0 else acc_ref[buf, 1] + r
        if v < n_dyn:
            last_lo = v == t
            if v == 0:
                acc_ref[buf, 0] = run
            else:
                acc_ref[buf, 0] = jnp.where(last_lo, run, acc_ref[buf, 0])
            run = jnp.where(last_lo, 0.0, run)
        acc_ref[buf, 1] = run

    def normalise(p):
        for tile in range(2):
            acc = acc_ref[p % 2, tile]
            o_full = acc[:, :128] / acc[:, 128:]
            pair = jnp.where(lane128 < HEAD_DIM, o_full[:nq], o_full[nq:])
            attn_ref[tile * nq:(tile + 1) * nq, p * 128:(p + 1) * 128] = pair.astype(BF16)

    for stage in range(N_PAIRS + 1):
        for v in range(N_VISITS):
            if stage < N_PAIRS:
                score_visit(stage, v)
            if stage > 0:
                value_visit(stage - 1, v)
        if stage > 0:
            normalise(stage - 1)
        if stage < N_PAIRS:
            row_max(stage)
    y_attn = _dot(attn_ref[...], wba_ref[...])

    h1 = jnp.concatenate([hlo_ref[0], hhi_ref[0]], axis=0)
    u = _rms_norm(h1, gmix_ref[...]).astype(BF16)
    gates = 0.5 * jnp.tanh(0.5 * _dot(u, wgate_ref[...])) + 0.5
    merged = gates[:, :D_MODEL] * y_pool + gates[:, D_MODEL:] * y_attn
    out = h1 + _dot(merged.astype(BF16), wout_ref[...])
    olo_ref[0] = out[:nq]
    ohi_ref[0] = out[nq:]


def _mixer(h1, p, q, k, v, kmean, g_mix, w_gate, pool_w, pool_scale, w_bp, w_ba, w_out):
    b, s, d = h1.shape
    nq = MOBA_BLOCK
    nt = s // nq
    assert nt == N_BLOCKS
    halo_per_tile = nq // MAX_POOL_HALO
    lo = lambda bi, t: (bi, t, 0)
    hi = lambda bi, t: (bi, nt - 1 - t, 0)
    halo_lo = lambda bi, t: (bi, jnp.maximum(t * halo_per_tile - 1, 0), 0)
    halo_hi = lambda bi, t: (bi, (nt - 1 - t) * halo_per_tile - 1, 0)
    seq = lambda bi, t: (bi, 0, 0)
    c2 = lambda bi, t: (0, 0)
    c3 = lambda bi, t: (0, 0, 0)
    one = pl.Buffered(1)
    half = jax.ShapeDtypeStruct((b, s // 2, d), F32)
    return pl.pallas_call(
        _mixer_kernel,
        grid=(b, nt // 2),
        in_specs=[
            pl.BlockSpec((1, nq, d), lo),
            pl.BlockSpec((1, nq, d), hi),
            pl.BlockSpec((1, nq, POOL_WIDTH), lo),
            pl.BlockSpec((1, nq, POOL_WIDTH), hi),
            pl.BlockSpec((1, MAX_POOL_HALO, POOL_WIDTH), halo_lo),
            pl.BlockSpec((1, MAX_POOL_HALO, POOL_WIDTH), halo_hi),
            pl.BlockSpec((1, nq, ATTN_WIDTH), lo),
            pl.BlockSpec((1, nq, ATTN_WIDTH), hi),
            pl.BlockSpec((1, s, ATTN_WIDTH), seq),
            pl.BlockSpec((1, s, ATTN_WIDTH), seq),
            pl.BlockSpec((1, nt, ATTN_WIDTH), seq),
            pl.BlockSpec((1, d), c2),
            pl.BlockSpec((d, 2 * d), c2, pipeline_mode=one),
            pl.BlockSpec((N_POOL_GROUPS, POOL_GROUP_DIM, POOL_GROUP_DIM), c3, pipeline_mode=one),
            pl.BlockSpec((1, POOL_WIDTH), c2),
            pl.BlockSpec((POOL_WIDTH, d), c2, pipeline_mode=one),
            pl.BlockSpec((ATTN_WIDTH, d), c2, pipeline_mode=one),
            pl.BlockSpec((d, d), c2, pipeline_mode=one),
        ],
        out_specs=[pl.BlockSpec((1, nq, d), lo), pl.BlockSpec((1, nq, d), lo)],
        out_shape=[half, half],
        scratch_shapes=[
            pltpu.VMEM((MAX_POOL_HALO + nq, POOL_WIDTH), F32),
            pltpu.VMEM((2, N_PAIRS, 2 * nq, 256), BF16),
            pltpu.VMEM((2, N_VISITS, 2 * nq, 128), F32),
            pltpu.VMEM((2, 2, 2 * nq, 128), F32),
            pltpu.VMEM((2, N_VISITS, 2 * nq, nq), F32),
            pltpu.VMEM((2, 2, 2 * nq, 256), F32),
            pltpu.VMEM((2 * nq, ATTN_WIDTH), BF16),
            pltpu.VMEM((2 * nq, POOL_WIDTH), BF16),
        ],
        compiler_params=pltpu.CompilerParams(
            dimension_semantics=("arbitrary", "arbitrary"),
            vmem_limit_bytes=VMEM_LIMIT_BYTES),
        name="mixer",
    )(h1, h1, p, p, p, p, q, q, k, v, kmean, g_mix, w_gate, pool_w, pool_scale,
      w_bp, w_ba, w_out)


def kernel(x, ffn1_norm, ffn1_w_gate, ffn1_w_up, ffn1_w_down, mix_norm, w_in,
           pool_w, pool_scale, w_branch_pool, w_branch_attn, w_out,
           ffn2_norm, ffn2_w_gate, ffn2_w_up, ffn2_w_down, final_norm):
    b, s, d = x.shape
    assert d == D_MODEL and s == N_BLOCKS * MOBA_BLOCK
    depth = ffn1_norm.shape[0]
    bf = lambda a: a.astype(BF16)
    h = x
    for l in range(depth):
        last = l == depth - 1
        h = _ffn(h.reshape(b * s, d), ffn1_norm[l][None], bf(ffn1_w_gate[l]),
                 bf(ffn1_w_up[l]), bf(ffn1_w_down[l]))
        w_l = bf(w_in[l])
        p, q, k, v, kmean = _inproj(h, mix_norm[l][None], w_l[:, :QKV_WIDTH])
        shp = lambda a: a.reshape(b, s, a.shape[-1])
        h_lo, h_hi = _mixer(h.reshape(b, s, d), shp(p), shp(q), shp(k), shp(v),
                            kmean.reshape(b, N_BLOCKS, ATTN_WIDTH),
                            mix_norm[l][None], w_l[:, QKV_WIDTH:], bf(pool_w[l]),
                            pool_scale[l][None], bf(w_branch_pool[l]),
                            bf(w_branch_attn[l]), bf(w_out[l]))
        h = _ffn_paired(h_lo, h_hi, ffn2_norm[l][None], bf(ffn2_w_gate[l]),
                        bf(ffn2_w_up[l]), bf(ffn2_w_down[l]), final_norm[None], last)
    return h
```

```python
import functools

import jax
import jax.numpy as jnp
from jax import lax
from jax.experimental import pallas as pl
from jax.experimental.pallas import tpu as pltpu

EPS = 1e-6
D_MODEL = 1024
D_FF = 2816
POOL_WIDTH = 512
N_POOL_GROUPS = 4
POOL_GROUP_DIM = 128
POOL_WINDOWS = (2, 4, 8, 16)
ATTN_WIDTH = 512
N_HEADS = 8
HEAD_DIM = 64
N_PAIRS = N_HEADS // 2
MOBA_BLOCK = 256
MOBA_TOPK = 3
N_BLOCKS = 8
N_VISITS = N_BLOCKS + 1
QKV_WIDTH = POOL_WIDTH + 3 * ATTN_WIDTH
NEG = -1e30

VMEM_LIMIT_BYTES = 58 * 1024 * 1024
FFN_TILE = 1024
FFN_SUB = 512
FFN_CHUNKS = ((0, 1024), (1024, 1024), (2048, 768))
INPROJ_TILE = 1024
MAX_POOL_HALO = 16
SUBLANES = 8
GATE_CHUNK = 2 * D_MODEL // (2 * N_POOL_GROUPS)

BF16 = jnp.bfloat16
F32 = jnp.float32


def _rms_norm(x, g):
    ms = jnp.mean(x * x, axis=-1, keepdims=True)
    return x * lax.rsqrt(ms + EPS) * g


def _dot(a, b):
    return jnp.dot(a, b, preferred_element_type=F32)


def _dot_nt(a, b):
    return lax.dot_general(a, b, (((1,), (1,)), ((), ())),
                           preferred_element_type=F32)


def _ffn_rows(xs, g_ref, wg_ref, wu_ref, wd_ref, gf_ref, store):
    def norm(x):
        return _rms_norm(x, g_ref[...]).astype(BF16)

    def finish(n, acc):
        h = xs[n] + 0.5 * acc
        if gf_ref is not None:
            h = _rms_norm(h, gf_ref[...])
        store(n, h)

    xn = norm(xs[0])
    pending = None
    for n in range(len(xs)):
        acc = None
        xn_next = None
        for ci, (start, size) in enumerate(FFN_CHUNKS):
            a = _dot(xn, wg_ref[:, start:start + size])
            b = _dot(xn, wu_ref[:, start:start + size])
            hmid = (a * jax.nn.sigmoid(a) * b).astype(BF16)
            d = _dot(hmid, wd_ref[start:start + size, :])
            acc = d if acc is None else acc + d
            if ci == 0:
                if pending is not None:
                    finish(*pending)
                if n + 1 < len(xs):
                    xn_next = norm(xs[n + 1])
        pending = (n, acc)
        xn = xn_next
    finish(*pending)


def _ffn_kernel(x_ref, g_ref, wg_ref, wu_ref, wd_ref, o_ref):
    groups = [slice(n * FFN_SUB, (n + 1) * FFN_SUB) for n in range(FFN_TILE // FFN_SUB)]

    def store(n, h):
        o_ref[groups[n], :] = h

    _ffn_rows([x_ref[rows, :] for rows in groups], g_ref, wg_ref, wu_ref, wd_ref, None, store)


def _ffn_weight_specs(d, const):
    one = pl.Buffered(1)
    return [
        pl.BlockSpec((1, d), const),
        pl.BlockSpec((d, D_FF), const, pipeline_mode=one),
        pl.BlockSpec((d, D_FF), const, pipeline_mode=one),
        pl.BlockSpec((D_FF, d), const, pipeline_mode=one),
    ]


def _ffn(x2d, g, wg, wu, wd):
    t, d = x2d.shape
    return pl.pallas_call(
        _ffn_kernel,
        grid=(t // FFN_TILE,),
        in_specs=[pl.BlockSpec((FFN_TILE, d), lambda i: (i, 0))]
        + _ffn_weight_specs(d, lambda i: (0, 0)),
        out_specs=pl.BlockSpec((FFN_TILE, d), lambda i: (i, 0)),
        out_shape=jax.ShapeDtypeStruct((t, d), F32),
        compiler_params=pltpu.CompilerParams(
            dimension_semantics=("arbitrary",),
            vmem_limit_bytes=VMEM_LIMIT_BYTES),
        name="ffn",
    )(x2d, g, wg, wu, wd)


def _ffn_paired_kernel(lo_ref, hi_ref, g_ref, wg_ref, wu_ref, wd_ref, gf_ref, o_ref,
                       *, final_norm, steps_lo):
    m = pl.program_id(1)
    n_tiles = FFN_TILE // MOBA_BLOCK
    per_group = FFN_SUB // MOBA_BLOCK
    xs = []
    for n in range(FFN_TILE // FFN_SUB):
        lo = lo_ref[0, n * FFN_SUB:(n + 1) * FFN_SUB, :]
        tiles = [n_tiles - 1 - (n * per_group + k) for k in range(per_group)]
        hi = jnp.concatenate(
            [hi_ref[0, i * MOBA_BLOCK:(i + 1) * MOBA_BLOCK, :] for i in tiles], axis=0)
        xs.append(jnp.where(m < steps_lo, lo, hi))

    def store(n, h):
        o_ref[0, n * FFN_SUB:(n + 1) * FFN_SUB, :] = h

    _ffn_rows(xs, g_ref, wg_ref, wu_ref, wd_ref, gf_ref if final_norm else None, store)


def _ffn_paired(h_lo, h_hi, g, wg, wu, wd, gf, final_norm):
    b, half_s, d = h_lo.shape
    steps_lo = half_s // FFN_TILE
    steps = 2 * steps_lo
    const = lambda bi, m: (0, 0)
    return pl.pallas_call(
        functools.partial(_ffn_paired_kernel, final_norm=final_norm, steps_lo=steps_lo),
        grid=(b, steps),
        in_specs=[
            pl.BlockSpec((1, FFN_TILE, d), lambda bi, m: (bi, jnp.minimum(m, steps_lo - 1), 0)),
            pl.BlockSpec((1, FFN_TILE, d),
                         lambda bi, m: (bi, jnp.clip(steps - 1 - m, 0, steps_lo - 1), 0)),
        ] + _ffn_weight_specs(d, const) + [pl.BlockSpec((1, d), const)],
        out_specs=pl.BlockSpec((1, FFN_TILE, d), lambda bi, m: (bi, m, 0)),
        out_shape=jax.ShapeDtypeStruct((b, 2 * half_s, d), F32),
        compiler_params=pltpu.CompilerParams(
            dimension_semantics=("arbitrary", "arbitrary"),
            vmem_limit_bytes=VMEM_LIMIT_BYTES),
        name="ffn_final" if final_norm else "ffn_paired",
    )(h_lo, h_hi, g, wg, wu, wd, gf)


def _inproj_kernel(h_ref, g_ref, w_ref, p_ref, q_ref, k_ref, v_ref, km_ref):
    u = _rms_norm(h_ref[...], g_ref[...]).astype(BF16)
    proj = _dot(u, w_ref[...])
    p_ref[...] = proj[:, :POOL_WIDTH]
    o = POOL_WIDTH
    q_ref[...] = (proj[:, o:o + ATTN_WIDTH] * (HEAD_DIM ** -0.5)).astype(BF16)
    k = proj[:, o + ATTN_WIDTH:o + 2 * ATTN_WIDTH]
    k_ref[...] = k.astype(BF16)
    v_ref[...] = proj[:, o + 2 * ATTN_WIDTH:o + 3 * ATTN_WIDTH].astype(BF16)
    for blk in range(INPROJ_TILE // MOBA_BLOCK):
        kb = k[blk * MOBA_BLOCK:(blk + 1) * MOBA_BLOCK, :]
        km_ref[0, blk:blk + 1, :] = jnp.mean(kb, axis=0, keepdims=True)


def _inproj(h2d, g, w_qkv):
    t, d = h2d.shape
    nblk = INPROJ_TILE // MOBA_BLOCK
    row = lambda i: (i, 0)
    const = lambda i: (0, 0)
    return pl.pallas_call(
        _inproj_kernel,
        grid=(t // INPROJ_TILE,),
        in_specs=[
            pl.BlockSpec((INPROJ_TILE, d), row),
            pl.BlockSpec((1, d), const),
            pl.BlockSpec((d, QKV_WIDTH), const, pipeline_mode=pl.Buffered(1)),
        ],
        out_specs=[
            pl.BlockSpec((INPROJ_TILE, POOL_WIDTH), row),
            pl.BlockSpec((INPROJ_TILE, ATTN_WIDTH), row),
            pl.BlockSpec((INPROJ_TILE, ATTN_WIDTH), row),
            pl.BlockSpec((INPROJ_TILE, ATTN_WIDTH), row),
            pl.BlockSpec((1, nblk, ATTN_WIDTH), lambda i: (i, 0, 0)),
        ],
        out_shape=[
            jax.ShapeDtypeStruct((t, POOL_WIDTH), F32),
            jax.ShapeDtypeStruct((t, ATTN_WIDTH), BF16),
            jax.ShapeDtypeStruct((t, ATTN_WIDTH), BF16),
            jax.ShapeDtypeStruct((t, ATTN_WIDTH), BF16),
            jax.ShapeDtypeStruct((t // INPROJ_TILE, nblk, ATTN_WIDTH), F32),
        ],
        compiler_params=pltpu.CompilerParams(
            dimension_semantics=("arbitrary",),
            vmem_limit_bytes=VMEM_LIMIT_BYTES),
        name="inproj",
    )(h2d, g, w_qkv)


def _pool_branch(i, p_ref, halo_ref, poolw_ref, pscale_ref, shift_ref, ypool_ref, row0, tick):
    nq = MOBA_BLOCK
    ext = MAX_POOL_HALO + nq
    shift_ref[:SUBLANES, :] = jnp.zeros((SUBLANES, POOL_GROUP_DIM), F32)
    t_glob = i * nq + lax.broadcasted_iota(jnp.int32, (nq, POOL_GROUP_DIM), 0)
    for g, w in enumerate(POOL_WINDOWS):
        cols = slice(g * POOL_GROUP_DIM, (g + 1) * POOL_GROUP_DIM)
        halo = halo_ref[0, :, cols]
        xg = p_ref[0, :, cols]
        cur = jnp.concatenate([jnp.where(i > 0, halo, jnp.zeros_like(halo)), xg], axis=0)
        span = 1
        while span < w:
            if span % SUBLANES == 0:
                shifted = jnp.concatenate(
                    [jnp.zeros((span, POOL_GROUP_DIM), F32), cur[:ext - span]], axis=0)
            else:
                shift_ref[SUBLANES:, :] = cur
                shifted = shift_ref[SUBLANES - span:SUBLANES - span + ext, :]
            cur = cur + shifted
            span *= 2
        count = jnp.minimum(t_glob + 1, w).astype(F32)
        mixed = (cur[MAX_POOL_HALO:] / count - xg).astype(BF16)
        yg = _dot(mixed, poolw_ref[g]) * pscale_ref[:, cols]
        ypool_ref[row0:row0 + nq, cols] = yg.astype(BF16)
        tick()


def _augmented_queries(i, q, km_stack, qaug_ref, tile):
    nq = MOBA_BLOCK
    lane128 = lax.broadcasted_iota(jnp.int32, (nq, 128), 1)
    gt = _dot_nt(km_stack, q)
    gate = gt[:64, :] + gt[64:, :]
    g_n = [gate[n * 8:(n + 1) * 8, :] for n in range(N_BLOCKS)]
    slope = jnp.exp2(-(lax.broadcasted_iota(jnp.int32, (N_HEADS, nq), 0) + 1).astype(F32))
    r_lane = lax.broadcasted_iota(jnp.int32, (N_HEADS, nq), 1).astype(F32)
    pre = []
    for n in range(N_BLOCKS):
        cnt = jnp.zeros((N_HEADS, nq), F32)
        for n2 in range(N_BLOCKS):
            if n2 == n:
                continue
            beats = (g_n[n2] >= g_n[n]) if n2 < n else (g_n[n2] > g_n[n])
            cnt = cnt + jnp.where(jnp.logical_and(beats, n2 < i), 1.0, 0.0)
        sel = jnp.logical_and(cnt < MOBA_TOPK, n < i)
        dist = (i - n).astype(F32) * float(MOBA_BLOCK)
        bias = jnp.where(sel, -slope * dist, NEG)
        pre.append(jnp.where(n == i, 0.0, bias))
    pre.append(-slope * r_lane)
    pre.append(slope)
    pre.append(jnp.zeros((128 - 80, nq), F32))
    q_ext_all = jnp.concatenate(pre, axis=0).T
    for p in range(N_PAIRS):
        qp = q[:, p * 128:(p + 1) * 128].astype(F32)
        for half in range(2):
            h = 2 * p + half
            in_head = (lane128 >= HEAD_DIM) if half else (lane128 < HEAD_DIM)
            qh = jnp.where(in_head, qp, 0.0).astype(BF16)
            ext_lane = jnp.logical_or(
                jnp.logical_and(lane128 < 64, lane128 % 8 == h),
                jnp.logical_or(lane128 == 64 + h, lane128 == 72 + h))
            qe = jnp.where(ext_lane, q_ext_all, 0.0).astype(BF16)
            rows = slice(half * nq, (half + 1) * nq)
            qaug_ref[tile, p, rows, :128] = qh
            qaug_ref[tile, p, rows, 128:] = qe


def _mixer_kernel(hlo_ref, hhi_ref, plo_ref, phi_ref, halo_lo_ref, halo_hi_ref,
                  qlo_ref, qhi_ref, k_ref, v_ref, km_ref,
                  gmix_ref, wgate_ref, poolw_ref, pscale_ref, wbp_ref, wba_ref, wout_ref,
                  olo_ref, ohi_ref,
                  shift_ref, qaug_ref, mxv_ref, m_ref, s_ref, acc_ref, attn_ref, ypool_ref,
                  u_ref, gates_ref):
    t = pl.program_id(1)
    nq = MOBA_BLOCK
    i_lo, i_hi = t, N_BLOCKS - 1 - t
    lane128 = lax.broadcasted_iota(jnp.int32, (nq, 128), 1)

    h1 = jnp.concatenate([hlo_ref[0], hhi_ref[0]], axis=0)
    u_ref[...] = _rms_norm(h1, gmix_ref[...]).astype(BF16)
    gate_chunks = iter(range(0, 2 * D_MODEL, GATE_CHUNK))

    def tick():
        c = next(gate_chunks)
        z = _dot(u_ref[...], wgate_ref[:, c:c + GATE_CHUNK])
        gates_ref[:, c:c + GATE_CHUNK] = 0.5 * jnp.tanh(0.5 * z) + 0.5

    _pool_branch(i_lo, plo_ref, halo_lo_ref, poolw_ref, pscale_ref, shift_ref, ypool_ref, 0, tick)
    _pool_branch(i_hi, phi_ref, halo_hi_ref, poolw_ref, pscale_ref, shift_ref, ypool_ref, nq, tick)
    y_pool = _dot(ypool_ref[...], wbp_ref[...])

    km = km_ref[0]
    head_of_lane = lax.broadcasted_iota(jnp.int32, (N_HEADS, ATTN_WIDTH), 1) // HEAD_DIM
    head_of_row = lax.broadcasted_iota(jnp.int32, (N_HEADS, ATTN_WIDTH), 0)
    head_mask = head_of_lane == head_of_row
    slabs = [jnp.where(head_mask, jnp.broadcast_to(km[n:n + 1, :], (N_HEADS, ATTN_WIDTH)), 0.0)
             for n in range(N_BLOCKS)]
    km_bd = jnp.concatenate(slabs, axis=0)
    km_hi = km_bd.astype(BF16)
    km_lo = (km_bd - km_hi.astype(F32)).astype(BF16)
    km_stack = jnp.concatenate([km_hi, km_lo], axis=0)
    _augmented_queries(i_lo, qlo_ref[0], km_stack, qaug_ref, 0)
    _augmented_queries(i_hi, qhi_ref[0], km_stack, qaug_ref, 1)

    key_row = lax.broadcasted_iota(jnp.int32, (nq, 128), 0)
    e_static = jnp.where(jnp.logical_and(lane128 >= 64, lane128 < 72), 1.0,
                         jnp.where(jnp.logical_and(lane128 >= 72, lane128 < 80),
                                   key_row.astype(F32), 0.0))
    ones_blk = jnp.ones((nq, 128), BF16)
    rr = lax.broadcasted_iota(jnp.int32, (2 * nq, nq), 0) % nq
    cc = lax.broadcasted_iota(jnp.int32, (2 * nq, nq), 1)
    causal_ok = cc <= rr

    n_dyn = N_BLOCKS // 2
    visits = []
    for v in range(N_VISITS):
        is_lo = v <= t
        tile = jnp.where(is_lo, 0, 1) if v < n_dyn else 1
        j = jnp.where(is_lo, v, v - t - 1)
        start = pl.multiple_of(j * nq, nq)
        e_j = jnp.where(jnp.logical_and(lane128 < 64, lane128 // 8 == j), 1.0,
                        e_static).astype(BF16)
        visits.append((is_lo, tile, start, e_j))

    def score_visit(p, v):
        _, tile, start, e_j = visits[v]
        cols = slice(p * 128, (p + 1) * 128)
        k_aug = jnp.concatenate([k_ref[0, pl.ds(start, nq), cols], e_j], axis=1)
        s = _dot_nt(qaug_ref[tile, p], k_aug)
        if v == N_VISITS - 1:
            s = jnp.where(causal_ok, s, NEG)
        elif v < n_dyn:
            s = jnp.where(jnp.logical_or(causal_ok, v != t), s, NEG)
        s_ref[p % 2, v] = s
        mxv_ref[p % 2, v] = jnp.maximum(s[:, :128], s[:, 128:])

    def row_max(p):
        buf = p % 2
        m_hi = mxv_ref[buf, n_dyn]
        for v in range(n_dyn + 1, N_VISITS):
            m_hi = jnp.maximum(m_hi, mxv_ref[buf, v])
        m_lo = jnp.full((2 * nq, 128), NEG, F32)
        for v in range(n_dyn):
            mv = mxv_ref[buf, v]
            m_lo = jnp.maximum(m_lo, jnp.where(visits[v][0], mv, NEG))
            m_hi = jnp.maximum(m_hi, jnp.where(visits[v][0], NEG, mv))
        m_ref[buf, 0] = jnp.broadcast_to(jnp.max(m_lo, axis=1, keepdims=True), (2 * nq, 128))
        m_ref[buf, 1] = jnp.broadcast_to(jnp.max(m_hi, axis=1, keepdims=True), (2 * nq, 128))

    def value_visit(p, v):
        buf = p % 2
        _, tile, start, _ = visits[v]
        cols = slice(p * 128, (p + 1) * 128)
        m = m_ref[buf, tile]
        pexp = jnp.exp(s_ref[buf, v] - jnp.concatenate([m, m], axis=1)).astype(BF16)
        v_ext = jnp.concatenate([v_ref[0, pl.ds(start, nq), cols], ones_blk], axis=1)
        r = _dot(pexp, v_ext)
        run = r if v == 0 else acc_ref[buf, 1] + r
        if v < n_dyn:
            last_lo = v == t
            if v == 0:
                acc_ref[buf, 0] = run
            else:
                acc_ref[buf, 0] = jnp.where(last_lo, run, acc_ref[buf, 0])
            run = jnp.where(last_lo, 0.0, run)
        acc_ref[buf, 1] = run

    def normalise(p):
        for tile in range(2):
            acc = acc_ref[p % 2, tile]
            o_full = acc[:, :128] / acc[:, 128:]
            pair = jnp.where(lane128 < HEAD_DIM, o_full[:nq], o_full[nq:])
            attn_ref[tile * nq:(tile + 1) * nq, p * 128:(p + 1) * 128] = pair.astype(BF16)

    for stage in range(N_PAIRS + 1):
        for v in range(N_VISITS):
            if stage < N_PAIRS:
                score_visit(stage, v)
            if stage > 0:
                value_visit(stage - 1, v)
        if stage > 0:
            normalise(stage - 1)
        if stage < N_PAIRS:
            row_max(stage)
    y_attn = _dot(attn_ref[...], wba_ref[...])

    merged = gates_ref[:, :D_MODEL] * y_pool + gates_ref[:, D_MODEL:] * y_attn
    out = h1 + _dot(merged.astype(BF16), wout_ref[...])
    olo_ref[0] = out[:nq]
    ohi_ref[0] = out[nq:]


def _mixer(h1, p, q, k, v, kmean, g_mix, w_gate, pool_w, pool_scale, w_bp, w_ba, w_out):
    b, s, d = h1.shape
    nq = MOBA_BLOCK
    nt = s // nq
    assert nt == N_BLOCKS
    halo_per_tile = nq // MAX_POOL_HALO
    lo = lambda bi, t: (bi, t, 0)
    hi = lambda bi, t: (bi, nt - 1 - t, 0)
    halo_lo = lambda bi, t: (bi, jnp.maximum(t * halo_per_tile - 1, 0), 0)
    halo_hi = lambda bi, t: (bi, (nt - 1 - t) * halo_per_tile - 1, 0)
    seq = lambda bi, t: (bi, 0, 0)
    c2 = lambda bi, t: (0, 0)
    c3 = lambda bi, t: (0, 0, 0)
    one = pl.Buffered(1)
    half = jax.ShapeDtypeStruct((b, s // 2, d), F32)
    return pl.pallas_call(
        _mixer_kernel,
        grid=(b, nt // 2),
        in_specs=[
            pl.BlockSpec((1, nq, d), lo),
            pl.BlockSpec((1, nq, d), hi),
            pl.BlockSpec((1, nq, POOL_WIDTH), lo),
            pl.BlockSpec((1, nq, POOL_WIDTH), hi),
            pl.BlockSpec((1, MAX_POOL_HALO, POOL_WIDTH), halo_lo),
            pl.BlockSpec((1, MAX_POOL_HALO, POOL_WIDTH), halo_hi),
            pl.BlockSpec((1, nq, ATTN_WIDTH), lo),
            pl.BlockSpec((1, nq, ATTN_WIDTH), hi),
            pl.BlockSpec((1, s, ATTN_WIDTH), seq),
            pl.BlockSpec((1, s, ATTN_WIDTH), seq),
            pl.BlockSpec((1, nt, ATTN_WIDTH), seq),
            pl.BlockSpec((1, d), c2),
            pl.BlockSpec((d, 2 * d), c2, pipeline_mode=one),
            pl.BlockSpec((N_POOL_GROUPS, POOL_GROUP_DIM, POOL_GROUP_DIM), c3, pipeline_mode=one),
            pl.BlockSpec((1, POOL_WIDTH), c2),
            pl.BlockSpec((POOL_WIDTH, d), c2, pipeline_mode=one),
            pl.BlockSpec((ATTN_WIDTH, d), c2, pipeline_mode=one),
            pl.BlockSpec((d, d), c2, pipeline_mode=one),
        ],
        out_specs=[pl.BlockSpec((1, nq, d), lo), pl.BlockSpec((1, nq, d), lo)],
        out_shape=[half, half],
        scratch_shapes=[
            pltpu.VMEM((SUBLANES + MAX_POOL_HALO + nq, POOL_GROUP_DIM), F32),
            pltpu.VMEM((2, N_PAIRS, 2 * nq, 256), BF16),
            pltpu.VMEM((2, N_VISITS, 2 * nq, 128), F32),
            pltpu.VMEM((2, 2, 2 * nq, 128), F32),
            pltpu.VMEM((2, N_VISITS, 2 * nq, nq), F32),
            pltpu.VMEM((2, 2, 2 * nq, 256), F32),
            pltpu.VMEM((2 * nq, ATTN_WIDTH), BF16),
            pltpu.VMEM((2 * nq, POOL_WIDTH), BF16),
            pltpu.VMEM((2 * nq, d), BF16),
            pltpu.VMEM((2 * nq, 2 * d), F32),
        ],
        compiler_params=pltpu.CompilerParams(
            dimension_semantics=("arbitrary", "arbitrary"),
            vmem_limit_bytes=VMEM_LIMIT_BYTES),
        name="mixer",
    )(h1, h1, p, p, p, p, q, q, k, v, kmean, g_mix, w_gate, pool_w, pool_scale,
      w_bp, w_ba, w_out)


def kernel(x, ffn1_norm, ffn1_w_gate, ffn1_w_up, ffn1_w_down, mix_norm, w_in,
           pool_w, pool_scale, w_branch_pool, w_branch_attn, w_out,
           ffn2_norm, ffn2_w_gate, ffn2_w_up, ffn2_w_down, final_norm):
    b, s, d = x.shape
    assert d == D_MODEL and s == N_BLOCKS * MOBA_BLOCK and (s // 2) % FFN_TILE == 0
    depth = ffn1_norm.shape[0]
    bf = lambda a: a.astype(BF16)
    h = x
    for l in range(depth):
        last = l == depth - 1
        h = _ffn(h.reshape(b * s, d), ffn1_norm[l][None], bf(ffn1_w_gate[l]),
                 bf(ffn1_w_up[l]), bf(ffn1_w_down[l]))
        w_l = bf(w_in[l])
        p, q, k, v, kmean = _inproj(h, mix_norm[l][None], w_l[:, :QKV_WIDTH])
        shp = lambda a: a.reshape(b, s, a.shape[-1])
        h_lo, h_hi = _mixer(h.reshape(b, s, d), shp(p), shp(q), shp(k), shp(v),
                            kmean.reshape(b, N_BLOCKS, ATTN_WIDTH),
                            mix_norm[l][None], w_l[:, QKV_WIDTH:], bf(pool_w[l]),
                            pool_scale[l][None], bf(w_branch_pool[l]),
                            bf(w_branch_attn[l]), bf(w_out[l]))
        h = _ffn_paired(h_lo, h_hi, ffn2_norm[l][None], bf(ffn2_w_gate[l]),
                        bf(ffn2_w_up[l]), bf(ffn2_w_down[l]), final_norm[None], last)
    return h
```

```python
import functools

import jax
import jax.numpy as jnp
from jax import lax
from jax.experimental import pallas as pl
from jax.experimental.pallas import tpu as pltpu

EPS = 1e-6
D_MODEL = 1024
D_FF = 2816
POOL_WIDTH = 512
N_POOL_GROUPS = 4
POOL_GROUP_DIM = 128
POOL_WINDOWS = (2, 4, 8, 16)
ATTN_WIDTH = 512
N_HEADS = 8
HEAD_DIM = 64
N_PAIRS = N_HEADS // 2
MOBA_BLOCK = 256
MOBA_TOPK = 3
N_BLOCKS = 8
N_VISITS = N_BLOCKS + 1
QKV_WIDTH = POOL_WIDTH + 3 * ATTN_WIDTH
NEG = -1e30

VMEM_LIMIT_BYTES = 58 * 1024 * 1024
FFN_TILE = 1024
FFN_SUB = 512
FFN_CHUNKS = ((0, 1024), (1024, 1024), (2048, 768))
INPROJ_TILE = 1024
MAX_POOL_HALO = 16
GATE_CHUNK = 2 * D_MODEL // (2 * N_POOL_GROUPS)

BF16 = jnp.bfloat16
F32 = jnp.float32


def _rms_norm(x, g):
    ms = jnp.mean(x * x, axis=-1, keepdims=True)
    return x * lax.rsqrt(ms + EPS) * g


def _dot(a, b):
    return jnp.dot(a, b, preferred_element_type=F32)


def _dot_nt(a, b):
    return lax.dot_general(a, b, (((1,), (1,)), ((), ())),
                           preferred_element_type=F32)


def _ffn_rows(xs, g_ref, wg_ref, wu_ref, wd_ref, gf_ref, store):
    def norm(x):
        return _rms_norm(x, g_ref[...]).astype(BF16)

    def finish(n, acc):
        h = xs[n] + 0.5 * acc
        if gf_ref is not None:
            h = _rms_norm(h, gf_ref[...])
        store(n, h)

    xn = norm(xs[0])
    pending = None
    for n in range(len(xs)):
        acc = None
        xn_next = None
        for ci, (start, size) in enumerate(FFN_CHUNKS):
            a = _dot(xn, wg_ref[:, start:start + size])
            b = _dot(xn, wu_ref[:, start:start + size])
            hmid = (a * jax.nn.sigmoid(a) * b).astype(BF16)
            d = _dot(hmid, wd_ref[start:start + size, :])
            acc = d if acc is None else acc + d
            if ci == 0:
                if pending is not None:
                    finish(*pending)
                if n + 1 < len(xs):
                    xn_next = norm(xs[n + 1])
        pending = (n, acc)
        xn = xn_next
    finish(*pending)


def _ffn_kernel(x_ref, g_ref, wg_ref, wu_ref, wd_ref, o_ref):
    groups = [slice(n * FFN_SUB, (n + 1) * FFN_SUB) for n in range(FFN_TILE // FFN_SUB)]

    def store(n, h):
        o_ref[groups[n], :] = h

    _ffn_rows([x_ref[rows, :] for rows in groups], g_ref, wg_ref, wu_ref, wd_ref, None, store)


def _ffn_weight_specs(d, const):
    one = pl.Buffered(1)
    return [
        pl.BlockSpec((1, d), const),
        pl.BlockSpec((d, D_FF), const, pipeline_mode=one),
        pl.BlockSpec((d, D_FF), const, pipeline_mode=one),
        pl.BlockSpec((D_FF, d), const, pipeline_mode=one),
    ]


def _ffn(x2d, g, wg, wu, wd):
    t, d = x2d.shape
    return pl.pallas_call(
        _ffn_kernel,
        grid=(t // FFN_TILE,),
        in_specs=[pl.BlockSpec((FFN_TILE, d), lambda i: (i, 0))]
        + _ffn_weight_specs(d, lambda i: (0, 0)),
        out_specs=pl.BlockSpec((FFN_TILE, d), lambda i: (i, 0)),
        out_shape=jax.ShapeDtypeStruct((t, d), F32),
        compiler_params=pltpu.CompilerParams(
            dimension_semantics=("arbitrary",),
            vmem_limit_bytes=VMEM_LIMIT_BYTES),
        name="ffn",
    )(x2d, g, wg, wu, wd)


def _ffn_paired_kernel(lo_ref, hi_ref, g_ref, wg_ref, wu_ref, wd_ref, gf_ref, o_ref,
                       *, final_norm, steps_lo):
    m = pl.program_id(1)
    n_tiles = FFN_TILE // MOBA_BLOCK
    per_group = FFN_SUB // MOBA_BLOCK
    xs = []
    for n in range(FFN_TILE // FFN_SUB):
        lo = lo_ref[0, n * FFN_SUB:(n + 1) * FFN_SUB, :]
        tiles = [n_tiles - 1 - (n * per_group + k) for k in range(per_group)]
        hi = jnp.concatenate(
            [hi_ref[0, i * MOBA_BLOCK:(i + 1) * MOBA_BLOCK, :] for i in tiles], axis=0)
        xs.append(jnp.where(m < steps_lo, lo, hi))

    def store(n, h):
        o_ref[0, n * FFN_SUB:(n + 1) * FFN_SUB, :] = h

    _ffn_rows(xs, g_ref, wg_ref, wu_ref, wd_ref, gf_ref if final_norm else None, store)


def _ffn_paired(h_lo, h_hi, g, wg, wu, wd, gf, final_norm):
    b, half_s, d = h_lo.shape
    steps_lo = half_s // FFN_TILE
    steps = 2 * steps_lo
    const = lambda bi, m: (0, 0)
    return pl.pallas_call(
        functools.partial(_ffn_paired_kernel, final_norm=final_norm, steps_lo=steps_lo),
        grid=(b, steps),
        in_specs=[
            pl.BlockSpec((1, FFN_TILE, d), lambda bi, m: (bi, jnp.minimum(m, steps_lo - 1), 0)),
            pl.BlockSpec((1, FFN_TILE, d),
                         lambda bi, m: (bi, jnp.clip(steps - 1 - m, 0, steps_lo - 1), 0)),
        ] + _ffn_weight_specs(d, const) + [pl.BlockSpec((1, d), const)],
        out_specs=pl.BlockSpec((1, FFN_TILE, d), lambda bi, m: (bi, m, 0)),
        out_shape=jax.ShapeDtypeStruct((b, 2 * half_s, d), F32),
        compiler_params=pltpu.CompilerParams(
            dimension_semantics=("arbitrary", "arbitrary"),
            vmem_limit_bytes=VMEM_LIMIT_BYTES),
        name="ffn_final" if final_norm else "ffn_paired",
    )(h_lo, h_hi, g, wg, wu, wd, gf)


def _inproj_kernel(h_ref, g_ref, w_ref, p_ref, q_ref, k_ref, v_ref, km_ref):
    u = _rms_norm(h_ref[...], g_ref[...]).astype(BF16)
    proj = _dot(u, w_ref[...])
    p_ref[...] = proj[:, :POOL_WIDTH]
    o = POOL_WIDTH
    q_ref[...] = (proj[:, o:o + ATTN_WIDTH] * (HEAD_DIM ** -0.5)).astype(BF16)
    k = proj[:, o + ATTN_WIDTH:o + 2 * ATTN_WIDTH]
    k_ref[...] = k.astype(BF16)
    v_ref[...] = proj[:, o + 2 * ATTN_WIDTH:o + 3 * ATTN_WIDTH].astype(BF16)
    for blk in range(INPROJ_TILE // MOBA_BLOCK):
        kb = k[blk * MOBA_BLOCK:(blk + 1) * MOBA_BLOCK, :]
        km_ref[0, blk:blk + 1, :] = jnp.mean(kb, axis=0, keepdims=True)


def _inproj(h2d, g, w_qkv):
    t, d = h2d.shape
    nblk = INPROJ_TILE // MOBA_BLOCK
    row = lambda i: (i, 0)
    const = lambda i: (0, 0)
    return pl.pallas_call(
        _inproj_kernel,
        grid=(t // INPROJ_TILE,),
        in_specs=[
            pl.BlockSpec((INPROJ_TILE, d), row),
            pl.BlockSpec((1, d), const),
            pl.BlockSpec((d, QKV_WIDTH), const, pipeline_mode=pl.Buffered(1)),
        ],
        out_specs=[
            pl.BlockSpec((INPROJ_TILE, POOL_WIDTH), row),
            pl.BlockSpec((INPROJ_TILE, ATTN_WIDTH), row),
            pl.BlockSpec((INPROJ_TILE, ATTN_WIDTH), row),
            pl.BlockSpec((INPROJ_TILE, ATTN_WIDTH), row),
            pl.BlockSpec((1, nblk, ATTN_WIDTH), lambda i: (i, 0, 0)),
        ],
        out_shape=[
            jax.ShapeDtypeStruct((t, POOL_WIDTH), F32),
            jax.ShapeDtypeStruct((t, ATTN_WIDTH), BF16),
            jax.ShapeDtypeStruct((t, ATTN_WIDTH), BF16),
            jax.ShapeDtypeStruct((t, ATTN_WIDTH), BF16),
            jax.ShapeDtypeStruct((t // INPROJ_TILE, nblk, ATTN_WIDTH), F32),
        ],
        compiler_params=pltpu.CompilerParams(
            dimension_semantics=("arbitrary",),
            vmem_limit_bytes=VMEM_LIMIT_BYTES),
        name="inproj",
    )(h2d, g, w_qkv)


def _pool_branch(i, p_ref, halo_ref, poolw_ref, pscale_ref, pext_ref, ypool_ref, row0, tick):
    nq = MOBA_BLOCK
    halo = halo_ref[0]
    pext_ref[:MAX_POOL_HALO, :] = jnp.where(i > 0, halo, jnp.zeros_like(halo))
    pext_ref[MAX_POOL_HALO:, :] = p_ref[0]
    t_glob = i * nq + lax.broadcasted_iota(jnp.int32, (nq, POOL_GROUP_DIM), 0)
    for g, w in enumerate(POOL_WINDOWS):
        cols = slice(g * POOL_GROUP_DIM, (g + 1) * POOL_GROUP_DIM)
        xg = pext_ref[MAX_POOL_HALO:, cols]
        win = xg
        for s in range(1, w):
            win = win + pext_ref[MAX_POOL_HALO - s:MAX_POOL_HALO - s + nq, cols]
        count = jnp.minimum(t_glob + 1, w).astype(F32)
        mixed = (win / count - xg).astype(BF16)
        yg = _dot(mixed, poolw_ref[g]) * pscale_ref[:, cols]
        ypool_ref[row0:row0 + nq, cols] = yg.astype(BF16)
        tick()


def _augmented_queries(i, q, km_stack, qaug_ref, tile):
    nq = MOBA_BLOCK
    lane128 = lax.broadcasted_iota(jnp.int32, (nq, 128), 1)
    gt = _dot_nt(km_stack, q)
    gate = gt[:64, :] + gt[64:, :]
    g_n = [gate[n * 8:(n + 1) * 8, :] for n in range(N_BLOCKS)]
    slope = jnp.exp2(-(lax.broadcasted_iota(jnp.int32, (N_HEADS, nq), 0) + 1).astype(F32))
    r_lane = lax.broadcasted_iota(jnp.int32, (N_HEADS, nq), 1).astype(F32)
    pre = []
    for n in range(N_BLOCKS):
        cnt = jnp.zeros((N_HEADS, nq), F32)
        for n2 in range(N_BLOCKS):
            if n2 == n:
                continue
            beats = (g_n[n2] >= g_n[n]) if n2 < n else (g_n[n2] > g_n[n])
            cnt = cnt + jnp.where(jnp.logical_and(beats, n2 < i), 1.0, 0.0)
        sel = jnp.logical_and(cnt < MOBA_TOPK, n < i)
        dist = (i - n).astype(F32) * float(MOBA_BLOCK)
        bias = jnp.where(sel, -slope * dist, NEG)
        pre.append(jnp.where(n == i, 0.0, bias))
    pre.append(-slope * r_lane)
    pre.append(slope)
    pre.append(jnp.zeros((128 - 80, nq), F32))
    q_ext_all = jnp.concatenate(pre, axis=0).T
    for p in range(N_PAIRS):
        qp = q[:, p * 128:(p + 1) * 128].astype(F32)
        for half in range(2):
            h = 2 * p + half
            in_head = (lane128 >= HEAD_DIM) if half else (lane128 < HEAD_DIM)
            qh = jnp.where(in_head, qp, 0.0).astype(BF16)
            ext_lane = jnp.logical_or(
                jnp.logical_and(lane128 < 64, lane128 % 8 == h),
                jnp.logical_or(lane128 == 64 + h, lane128 == 72 + h))
            qe = jnp.where(ext_lane, q_ext_all, 0.0).astype(BF16)
            rows = slice(half * nq, (half + 1) * nq)
            qaug_ref[tile, p, rows, :128] = qh
            qaug_ref[tile, p, rows, 128:] = qe


def _mixer_kernel(hlo_ref, hhi_ref, plo_ref, phi_ref, halo_lo_ref, halo_hi_ref,
                  qlo_ref, qhi_ref, k_ref, v_ref, km_ref,
                  gmix_ref, wgate_ref, poolw_ref, pscale_ref, wbp_ref, wba_ref, wout_ref,
                  olo_ref, ohi_ref,
                  pext_ref, qaug_ref, mxv_ref, m_ref, s_ref, acc_ref, attn_ref, ypool_ref,
                  u_ref, gates_ref):
    t = pl.program_id(1)
    nq = MOBA_BLOCK
    i_lo, i_hi = t, N_BLOCKS - 1 - t
    lane128 = lax.broadcasted_iota(jnp.int32, (nq, 128), 1)

    h1 = jnp.concatenate([hlo_ref[0], hhi_ref[0]], axis=0)
    u_ref[...] = _rms_norm(h1, gmix_ref[...]).astype(BF16)
    gate_chunks = iter(range(0, 2 * D_MODEL, GATE_CHUNK))

    def tick():
        c = next(gate_chunks)
        z = _dot(u_ref[...], wgate_ref[:, c:c + GATE_CHUNK])
        gates_ref[:, c:c + GATE_CHUNK] = 0.5 * jnp.tanh(0.5 * z) + 0.5

    _pool_branch(i_lo, plo_ref, halo_lo_ref, poolw_ref, pscale_ref, pext_ref, ypool_ref, 0, tick)
    _pool_branch(i_hi, phi_ref, halo_hi_ref, poolw_ref, pscale_ref, pext_ref, ypool_ref, nq, tick)
    y_pool = _dot(ypool_ref[...], wbp_ref[...])

    km = km_ref[0]
    head_of_lane = lax.broadcasted_iota(jnp.int32, (N_HEADS, ATTN_WIDTH), 1) // HEAD_DIM
    head_of_row = lax.broadcasted_iota(jnp.int32, (N_HEADS, ATTN_WIDTH), 0)
    head_mask = head_of_lane == head_of_row
    slabs = [jnp.where(head_mask, jnp.broadcast_to(km[n:n + 1, :], (N_HEADS, ATTN_WIDTH)), 0.0)
             for n in range(N_BLOCKS)]
    km_bd = jnp.concatenate(slabs, axis=0)
    km_hi = km_bd.astype(BF16)
    km_lo = (km_bd - km_hi.astype(F32)).astype(BF16)
    km_stack = jnp.concatenate([km_hi, km_lo], axis=0)
    _augmented_queries(i_lo, qlo_ref[0], km_stack, qaug_ref, 0)
    _augmented_queries(i_hi, qhi_ref[0], km_stack, qaug_ref, 1)

    key_row = lax.broadcasted_iota(jnp.int32, (nq, 128), 0)
    e_static = jnp.where(jnp.logical_and(lane128 >= 64, lane128 < 72), 1.0,
                         jnp.where(jnp.logical_and(lane128 >= 72, lane128 < 80),
                                   key_row.astype(F32), 0.0))
    ones_blk = jnp.ones((nq, 128), BF16)
    rr = lax.broadcasted_iota(jnp.int32, (2 * nq, nq), 0) % nq
    cc = lax.broadcasted_iota(jnp.int32, (2 * nq, nq), 1)
    causal_ok = cc <= rr

    n_dyn = N_BLOCKS // 2
    visits = []
    for v in range(N_VISITS):
        is_lo = v <= t
        tile = jnp.where(is_lo, 0, 1) if v < n_dyn else 1
        j = jnp.where(is_lo, v, v - t - 1)
        start = pl.multiple_of(j * nq, nq)
        e_j = jnp.where(jnp.logical_and(lane128 < 64, lane128 // 8 == j), 1.0,
                        e_static).astype(BF16)
        visits.append((is_lo, tile, start, e_j))

    def score_visit(p, v):
        _, tile, start, e_j = visits[v]
        cols = slice(p * 128, (p + 1) * 128)
        k_aug = jnp.concatenate([k_ref[0, pl.ds(start, nq), cols], e_j], axis=1)
        s = _dot_nt(qaug_ref[tile, p], k_aug)
        if v == N_VISITS - 1:
            s = jnp.where(causal_ok, s, NEG)
        elif v < n_dyn:
            s = jnp.where(jnp.logical_or(causal_ok, v != t), s, NEG)
        s_ref[p % 2, v] = s
        mxv_ref[p % 2, v] = jnp.maximum(s[:, :128], s[:, 128:])

    def row_max(p):
        buf = p % 2
        m_hi = mxv_ref[buf, n_dyn]
        for v in range(n_dyn + 1, N_VISITS):
            m_hi = jnp.maximum(m_hi, mxv_ref[buf, v])
        m_lo = jnp.full((2 * nq, 128), NEG, F32)
        for v in range(n_dyn):
            mv = mxv_ref[buf, v]
            m_lo = jnp.maximum(m_lo, jnp.where(visits[v][0], mv, NEG))
            m_hi = jnp.maximum(m_hi, jnp.where(visits[v][0], NEG, mv))
        m_ref[buf, 0] = jnp.broadcast_to(jnp.max(m_lo, axis=1, keepdims=True), (2 * nq, 128))
        m_ref[buf, 1] = jnp.broadcast_to(jnp.max(m_hi, axis=1, keepdims=True), (2 * nq, 128))

    def value_visit(p, v):
        buf = p % 2
        _, tile, start, _ = visits[v]
        cols = slice(p * 128, (p + 1) * 128)
        m = m_ref[buf, tile]
        pexp = jnp.exp(s_ref[buf, v] - jnp.concatenate([m, m], axis=1)).astype(BF16)
        v_ext = jnp.concatenate([v_ref[0, pl.ds(start, nq), cols], ones_blk], axis=1)
        r = _dot(pexp, v_ext)
        run = r if v == 0 else acc_ref[buf, 1] + r
        if v < n_dyn:
            last_lo = v == t
            if v == 0:
                acc_ref[buf, 0] = run
            else:
                acc_ref[buf, 0] = jnp.where(last_lo, run, acc_ref[buf, 0])
            run = jnp.where(last_lo, 0.0, run)
        acc_ref[buf, 1] = run

    def normalise(p):
        for tile in range(2):
            acc = acc_ref[p % 2, tile]
            o_full = acc[:, :128] / acc[:, 128:]
            pair = jnp.where(lane128 < HEAD_DIM, o_full[:nq], o_full[nq:])
            attn_ref[tile * nq:(tile + 1) * nq, p * 128:(p + 1) * 128] = pair.astype(BF16)

    for stage in range(N_PAIRS + 1):
        for v in range(N_VISITS):
            if stage < N_PAIRS:
                score_visit(stage, v)
            if stage > 0:
                value_visit(stage - 1, v)
        if stage > 0:
            normalise(stage - 1)
        if stage < N_PAIRS:
            row_max(stage)
    y_attn = _dot(attn_ref[...], wba_ref[...])

    merged = gates_ref[:, :D_MODEL] * y_pool + gates_ref[:, D_MODEL:] * y_attn
    out = h1 + _dot(merged.astype(BF16), wout_ref[...])
    olo_ref[0] = out[:nq]
    ohi_ref[0] = out[nq:]


def _mixer(h1, p, q, k, v, kmean, g_mix, w_gate, pool_w, pool_scale, w_bp, w_ba, w_out):
    b, s, d = h1.shape
    nq = MOBA_BLOCK
    nt = s // nq
    assert nt == N_BLOCKS
    halo_per_tile = nq // MAX_POOL_HALO
    lo = lambda bi, t: (bi, t, 0)
    hi = lambda bi, t: (bi, nt - 1 - t, 0)
    halo_lo = lambda bi, t: (bi, jnp.maximum(t * halo_per_tile - 1, 0), 0)
    halo_hi = lambda bi, t: (bi, (nt - 1 - t) * halo_per_tile - 1, 0)
    seq = lambda bi, t: (bi, 0, 0)
    c2 = lambda bi, t: (0, 0)
    c3 = lambda bi, t: (0, 0, 0)
    one = pl.Buffered(1)
    half = jax.ShapeDtypeStruct((b, s // 2, d), F32)
    return pl.pallas_call(
        _mixer_kernel,
        grid=(b, nt // 2),
        in_specs=[
            pl.BlockSpec((1, nq, d), lo),
            pl.BlockSpec((1, nq, d), hi),
            pl.BlockSpec((1, nq, POOL_WIDTH), lo),
            pl.BlockSpec((1, nq, POOL_WIDTH), hi),
            pl.BlockSpec((1, MAX_POOL_HALO, POOL_WIDTH), halo_lo),
            pl.BlockSpec((1, MAX_POOL_HALO, POOL_WIDTH), halo_hi),
            pl.BlockSpec((1, nq, ATTN_WIDTH), lo),
            pl.BlockSpec((1, nq, ATTN_WIDTH), hi),
            pl.BlockSpec((1, s, ATTN_WIDTH), seq),
            pl.BlockSpec((1, s, ATTN_WIDTH), seq),
            pl.BlockSpec((1, nt, ATTN_WIDTH), seq),
            pl.BlockSpec((1, d), c2),
            pl.BlockSpec((d, 2 * d), c2, pipeline_mode=one),
            pl.BlockSpec((N_POOL_GROUPS, POOL_GROUP_DIM, POOL_GROUP_DIM), c3, pipeline_mode=one),
            pl.BlockSpec((1, POOL_WIDTH), c2),
            pl.BlockSpec((POOL_WIDTH, d), c2, pipeline_mode=one),
            pl.BlockSpec((ATTN_WIDTH, d), c2, pipeline_mode=one),
            pl.BlockSpec((d, d), c2, pipeline_mode=one),
        ],
        out_specs=[pl.BlockSpec((1, nq, d), lo), pl.BlockSpec((1, nq, d), lo)],
        out_shape=[half, half],
        scratch_shapes=[
            pltpu.VMEM((MAX_POOL_HALO + nq, POOL_WIDTH), F32),
            pltpu.VMEM((2, N_PAIRS, 2 * nq, 256), BF16),
            pltpu.VMEM((2, N_VISITS, 2 * nq, 128), F32),
            pltpu.VMEM((2, 2, 2 * nq, 128), F32),
            pltpu.VMEM((2, N_VISITS, 2 * nq, nq), F32),
            pltpu.VMEM((2, 2, 2 * nq, 256), F32),
            pltpu.VMEM((2 * nq, ATTN_WIDTH), BF16),
            pltpu.VMEM((2 * nq, POOL_WIDTH), BF16),
            pltpu.VMEM((2 * nq, d), BF16),
            pltpu.VMEM((2 * nq, 2 * d), F32),
        ],
        compiler_params=pltpu.CompilerParams(
            dimension_semantics=("arbitrary", "arbitrary"),
            vmem_limit_bytes=VMEM_LIMIT_BYTES),
        name="mixer",
    )(h1, h1, p, p, p, p, q, q, k, v, kmean, g_mix, w_gate, pool_w, pool_scale,
      w_bp, w_ba, w_out)


def kernel(x, ffn1_norm, ffn1_w_gate, ffn1_w_up, ffn1_w_down, mix_norm, w_in,
           pool_w, pool_scale, w_branch_pool, w_branch_attn, w_out,
           ffn2_norm, ffn2_w_gate, ffn2_w_up, ffn2_w_down, final_norm):
    b, s, d = x.shape
    assert d == D_MODEL and s == N_BLOCKS * MOBA_BLOCK and (s // 2) % FFN_TILE == 0
    depth = ffn1_norm.shape[0]
    bf = lambda a: a.astype(BF16)
    h = x
    for l in range(depth):
        last = l == depth - 1
        h = _ffn(h.reshape(b * s, d), ffn1_norm[l][None], bf(ffn1_w_gate[l]),
                 bf(ffn1_w_up[l]), bf(ffn1_w_down[l]))
        w_l = bf(w_in[l])
        p, q, k, v, kmean = _inproj(h, mix_norm[l][None], w_l[:, :QKV_WIDTH])
        shp = lambda a: a.reshape(b, s, a.shape[-1])
        h_lo, h_hi = _mixer(h.reshape(b, s, d), shp(p), shp(q), shp(k), shp(v),
                            kmean.reshape(b, N_BLOCKS, ATTN_WIDTH),
                            mix_norm[l][None], w_l[:, QKV_WIDTH:], bf(pool_w[l]),
                            pool_scale[l][None], bf(w_branch_pool[l]),
                            bf(w_branch_attn[l]), bf(w_out[l]))
        h = _ffn_paired(h_lo, h_hi, ffn2_norm[l][None], bf(ffn2_w_gate[l]),
                        bf(ffn2_w_up[l]), bf(ffn2_w_down[l]), final_norm[None], last)
    return h
```

```python
import functools

import jax
import jax.numpy as jnp
from jax import lax
from jax.experimental import pallas as pl
from jax.experimental.pallas import tpu as pltpu

EPS = 1e-6
D_MODEL = 1024
D_FF = 2816
POOL_WIDTH = 512
N_POOL_GROUPS = 4
POOL_GROUP_DIM = 128
POOL_WINDOWS = (2, 4, 8, 16)
ATTN_WIDTH = 512
N_HEADS = 8
HEAD_DIM = 64
N_PAIRS = N_HEADS // 2
MOBA_BLOCK = 256
MOBA_TOPK = 3
N_BLOCKS = 8
N_VISITS = N_BLOCKS + 1
QKV_WIDTH = POOL_WIDTH + 3 * ATTN_WIDTH
NEG = -1e30

VMEM_LIMIT_BYTES = 58 * 1024 * 1024
FFN_TILE = N_BLOCKS * MOBA_BLOCK
FFN_SUB = 512
FFN_CHUNKS = ((0, 1024), (1024, 1024), (2048, 768))
INPROJ_TILE = 1024
MAX_POOL_HALO = 16
GATE_CHUNK = 2 * D_MODEL // (2 * N_POOL_GROUPS)

BF16 = jnp.bfloat16
F32 = jnp.float32


def _rms_norm(x, g):
    ms = jnp.mean(x * x, axis=-1, keepdims=True)
    return x * lax.rsqrt(ms + EPS) * g


def _dot(a, b):
    return jnp.dot(a, b, preferred_element_type=F32)


def _dot_nt(a, b):
    return lax.dot_general(a, b, (((1,), (1,)), ((), ())),
                           preferred_element_type=F32)


def _ffn_rows(xs, g_ref, wg_ref, wu_ref, wd_ref, gf_ref, store):
    def norm(x):
        return _rms_norm(x, g_ref[...]).astype(BF16)

    def finish(n, acc):
        h = xs[n] + 0.5 * acc
        if gf_ref is not None:
            h = _rms_norm(h, gf_ref[...])
        store(n, h)

    xn = norm(xs[0])
    pending = None
    for n in range(len(xs)):
        acc = None
        xn_next = None
        for ci, (start, size) in enumerate(FFN_CHUNKS):
            a = _dot(xn, wg_ref[:, start:start + size])
            b = _dot(xn, wu_ref[:, start:start + size])
            hmid = (a * jax.nn.sigmoid(a) * b).astype(BF16)
            d = _dot(hmid, wd_ref[start:start + size, :])
            acc = d if acc is None else acc + d
            if ci == 0:
                if pending is not None:
                    finish(*pending)
                if n + 1 < len(xs):
                    xn_next = norm(xs[n + 1])
        pending = (n, acc)
        xn = xn_next
    finish(*pending)


def _ffn_kernel(x_ref, g_ref, wg_ref, wu_ref, wd_ref, o_ref):
    groups = [slice(n * FFN_SUB, (n + 1) * FFN_SUB) for n in range(FFN_TILE // FFN_SUB)]

    def store(n, h):
        o_ref[groups[n], :] = h

    _ffn_rows([x_ref[rows, :] for rows in groups], g_ref, wg_ref, wu_ref, wd_ref, None, store)


def _ffn_weight_specs(d, const):
    one = pl.Buffered(1)
    return [
        pl.BlockSpec((1, d), const),
        pl.BlockSpec((d, D_FF), const, pipeline_mode=one),
        pl.BlockSpec((d, D_FF), const, pipeline_mode=one),
        pl.BlockSpec((D_FF, d), const, pipeline_mode=one),
    ]


def _ffn(x2d, g, wg, wu, wd):
    t, d = x2d.shape
    return pl.pallas_call(
        _ffn_kernel,
        grid=(t // FFN_TILE,),
        in_specs=[pl.BlockSpec((FFN_TILE, d), lambda i: (i, 0))]
        + _ffn_weight_specs(d, lambda i: (0, 0)),
        out_specs=pl.BlockSpec((FFN_TILE, d), lambda i: (i, 0)),
        out_shape=jax.ShapeDtypeStruct((t, d), F32),
        compiler_params=pltpu.CompilerParams(
            dimension_semantics=("arbitrary",),
            vmem_limit_bytes=VMEM_LIMIT_BYTES),
        name="ffn",
    )(x2d, g, wg, wu, wd)


def _ffn_paired_kernel(lo_ref, hi_ref, g_ref, wg_ref, wu_ref, wd_ref, gf_ref, o_ref,
                       *, final_norm):
    half_s = lo_ref.shape[1]
    n_hi_tiles = half_s // MOBA_BLOCK
    per_group = FFN_SUB // MOBA_BLOCK
    xs = [lo_ref[0, n * FFN_SUB:(n + 1) * FFN_SUB, :] for n in range(half_s // FFN_SUB)]
    for n in range(half_s // FFN_SUB):
        tiles = [n_hi_tiles - 1 - (n * per_group + k) for k in range(per_group)]
        xs.append(jnp.concatenate(
            [hi_ref[0, i * MOBA_BLOCK:(i + 1) * MOBA_BLOCK, :] for i in tiles], axis=0))

    def store(n, h):
        o_ref[0, n * FFN_SUB:(n + 1) * FFN_SUB, :] = h

    _ffn_rows(xs, g_ref, wg_ref, wu_ref, wd_ref, gf_ref if final_norm else None, store)


def _ffn_paired(h_lo, h_hi, g, wg, wu, wd, gf, final_norm):
    b, half_s, d = h_lo.shape
    assert 2 * half_s == FFN_TILE and half_s % FFN_SUB == 0
    const = lambda bi: (0, 0)
    seq = lambda bi: (bi, 0, 0)
    return pl.pallas_call(
        functools.partial(_ffn_paired_kernel, final_norm=final_norm),
        grid=(b,),
        in_specs=[pl.BlockSpec((1, half_s, d), seq), pl.BlockSpec((1, half_s, d), seq)]
        + _ffn_weight_specs(d, const) + [pl.BlockSpec((1, d), const)],
        out_specs=pl.BlockSpec((1, FFN_TILE, d), seq),
        out_shape=jax.ShapeDtypeStruct((b, FFN_TILE, d), F32),
        compiler_params=pltpu.CompilerParams(
            dimension_semantics=("arbitrary",),
            vmem_limit_bytes=VMEM_LIMIT_BYTES),
        name="ffn_final" if final_norm else "ffn_paired",
    )(h_lo, h_hi, g, wg, wu, wd, gf)


def _inproj_kernel(h_ref, g_ref, w_ref, p_ref, q_ref, kt_ref, v_ref, km_ref):
    u = _rms_norm(h_ref[...], g_ref[...]).astype(BF16)
    proj = _dot(u, w_ref[...])
    p_ref[...] = proj[:, :POOL_WIDTH]
    o = POOL_WIDTH
    q_ref[...] = (proj[:, o:o + ATTN_WIDTH] * (HEAD_DIM ** -0.5)).astype(BF16)
    k = proj[:, o + ATTN_WIDTH:o + 2 * ATTN_WIDTH]
    v_ref[...] = proj[:, o + 2 * ATTN_WIDTH:o + 3 * ATTN_WIDTH].astype(BF16)
    for blk in range(INPROJ_TILE // MOBA_BLOCK):
        kb = k[blk * MOBA_BLOCK:(blk + 1) * MOBA_BLOCK, :]
        kt_ref[blk] = kb.T.astype(BF16)
        km_ref[0, blk:blk + 1, :] = jnp.mean(kb, axis=0, keepdims=True)


def _inproj(h2d, g, w_qkv):
    t, d = h2d.shape
    nblk = INPROJ_TILE // MOBA_BLOCK
    row = lambda i: (i, 0)
    const = lambda i: (0, 0)
    return pl.pallas_call(
        _inproj_kernel,
        grid=(t // INPROJ_TILE,),
        in_specs=[
            pl.BlockSpec((INPROJ_TILE, d), row),
            pl.BlockSpec((1, d), const),
            pl.BlockSpec((d, QKV_WIDTH), const, pipeline_mode=pl.Buffered(1)),
        ],
        out_specs=[
            pl.BlockSpec((INPROJ_TILE, POOL_WIDTH), row),
            pl.BlockSpec((INPROJ_TILE, ATTN_WIDTH), row),
            pl.BlockSpec((nblk, ATTN_WIDTH, MOBA_BLOCK), lambda i: (i, 0, 0)),
            pl.BlockSpec((INPROJ_TILE, ATTN_WIDTH), row),
            pl.BlockSpec((1, nblk, ATTN_WIDTH), lambda i: (i, 0, 0)),
        ],
        out_shape=[
            jax.ShapeDtypeStruct((t, POOL_WIDTH), F32),
            jax.ShapeDtypeStruct((t, ATTN_WIDTH), BF16),
            jax.ShapeDtypeStruct((t // MOBA_BLOCK, ATTN_WIDTH, MOBA_BLOCK), BF16),
            jax.ShapeDtypeStruct((t, ATTN_WIDTH), BF16),
            jax.ShapeDtypeStruct((t // INPROJ_TILE, nblk, ATTN_WIDTH), F32),
        ],
        compiler_params=pltpu.CompilerParams(
            dimension_semantics=("arbitrary",),
            vmem_limit_bytes=VMEM_LIMIT_BYTES),
        name="inproj",
    )(h2d, g, w_qkv)


def _pool_branch(i, p_ref, halo_ref, poolw_ref, pscale_ref, pext_ref, ypool_ref, row0, tick):
    nq = MOBA_BLOCK
    halo = halo_ref[0]
    pext_ref[:MAX_POOL_HALO, :] = jnp.where(i > 0, halo, jnp.zeros_like(halo))
    pext_ref[MAX_POOL_HALO:, :] = p_ref[0]
    t_glob = i * nq + lax.broadcasted_iota(jnp.int32, (nq, POOL_GROUP_DIM), 0)
    for g, w in enumerate(POOL_WINDOWS):
        cols = slice(g * POOL_GROUP_DIM, (g + 1) * POOL_GROUP_DIM)
        xg = pext_ref[MAX_POOL_HALO:, cols]
        win = xg
        for s in range(1, w):
            win = win + pext_ref[MAX_POOL_HALO - s:MAX_POOL_HALO - s + nq, cols]
        count = jnp.minimum(t_glob + 1, w).astype(F32)
        mixed = (win / count - xg).astype(BF16)
        yg = _dot(mixed, poolw_ref[g]) * pscale_ref[:, cols]
        ypool_ref[row0:row0 + nq, cols] = yg.astype(BF16)
        tick()


def _augmented_queries(i, q, km_stack, qaug_ref, tile):
    nq = MOBA_BLOCK
    lane128 = lax.broadcasted_iota(jnp.int32, (nq, 128), 1)
    gt = _dot_nt(km_stack, q)
    gate = gt[:64, :] + gt[64:, :]
    g_n = [gate[n * 8:(n + 1) * 8, :] for n in range(N_BLOCKS)]
    slope = jnp.exp2(-(lax.broadcasted_iota(jnp.int32, (N_HEADS, nq), 0) + 1).astype(F32))
    r_lane = lax.broadcasted_iota(jnp.int32, (N_HEADS, nq), 1).astype(F32)
    pre = []
    for n in range(N_BLOCKS):
        cnt = jnp.zeros((N_HEADS, nq), F32)
        for n2 in range(N_BLOCKS):
            if n2 == n:
                continue
            beats = (g_n[n2] >= g_n[n]) if n2 < n else (g_n[n2] > g_n[n])
            cnt = cnt + jnp.where(jnp.logical_and(beats, n2 < i), 1.0, 0.0)
        sel = jnp.logical_and(cnt < MOBA_TOPK, n < i)
        dist = (i - n).astype(F32) * float(MOBA_BLOCK)
        bias = jnp.where(sel, -slope * dist, NEG)
        pre.append(jnp.where(n == i, 0.0, bias))
    pre.append(-slope * r_lane)
    pre.append(slope)
    pre.append(jnp.zeros((128 - 80, nq), F32))
    q_ext_all = jnp.concatenate(pre, axis=0).T
    for p in range(N_PAIRS):
        qp = q[:, p * 128:(p + 1) * 128].astype(F32)
        for half in range(2):
            h = 2 * p + half
            in_head = (lane128 >= HEAD_DIM) if half else (lane128 < HEAD_DIM)
            qh = jnp.where(in_head, qp, 0.0).astype(BF16)
            ext_lane = jnp.logical_or(
                jnp.logical_and(lane128 < 64, lane128 % 8 == h),
                jnp.logical_or(lane128 == 64 + h, lane128 == 72 + h))
            qe = jnp.where(ext_lane, q_ext_all, 0.0).astype(BF16)
            rows = slice(half * nq, (half + 1) * nq)
            qaug_ref[tile, p, rows, :128] = qh
            qaug_ref[tile, p, rows, 128:] = qe


def _mixer_kernel(hlo_ref, hhi_ref, plo_ref, phi_ref, halo_lo_ref, halo_hi_ref,
                  qlo_ref, qhi_ref, kt_ref, v_ref, km_ref,
                  gmix_ref, wgate_ref, poolw_ref, pscale_ref, wbp_ref, wba_ref, wout_ref,
                  olo_ref, ohi_ref,
                  pext_ref, qaug_ref, mxv_ref, m_ref, s_ref, acc_ref, attn_ref, ypool_ref,
                  u_ref, gates_ref):
    t = pl.program_id(1)
    nq = MOBA_BLOCK
    i_lo, i_hi = t, N_BLOCKS - 1 - t
    lane128 = lax.broadcasted_iota(jnp.int32, (nq, 128), 1)

    h1 = jnp.concatenate([hlo_ref[0], hhi_ref[0]], axis=0)
    u_ref[...] = _rms_norm(h1, gmix_ref[...]).astype(BF16)
    gate_chunks = iter(range(0, 2 * D_MODEL, GATE_CHUNK))

    def tick():
        c = next(gate_chunks)
        z = _dot(u_ref[...], wgate_ref[:, c:c + GATE_CHUNK])
        gates_ref[:, c:c + GATE_CHUNK] = 0.5 * jnp.tanh(0.5 * z) + 0.5

    _pool_branch(i_lo, plo_ref, halo_lo_ref, poolw_ref, pscale_ref, pext_ref, ypool_ref, 0, tick)
    _pool_branch(i_hi, phi_ref, halo_hi_ref, poolw_ref, pscale_ref, pext_ref, ypool_ref, nq, tick)
    y_pool = _dot(ypool_ref[...], wbp_ref[...])

    km = km_ref[0]
    head_of_lane = lax.broadcasted_iota(jnp.int32, (N_HEADS, ATTN_WIDTH), 1) // HEAD_DIM
    head_of_row = lax.broadcasted_iota(jnp.int32, (N_HEADS, ATTN_WIDTH), 0)
    head_mask = head_of_lane == head_of_row
    slabs = [jnp.where(head_mask, jnp.broadcast_to(km[n:n + 1, :], (N_HEADS, ATTN_WIDTH)), 0.0)
             for n in range(N_BLOCKS)]
    km_bd = jnp.concatenate(slabs, axis=0)
    km_hi = km_bd.astype(BF16)
    km_lo = (km_bd - km_hi.astype(F32)).astype(BF16)
    km_stack = jnp.concatenate([km_hi, km_lo], axis=0)
    _augmented_queries(i_lo, qlo_ref[0], km_stack, qaug_ref, 0)
    _augmented_queries(i_hi, qhi_ref[0], km_stack, qaug_ref, 1)

    ext_row = lax.broadcasted_iota(jnp.int32, (128, nq), 0)
    key_col = lax.broadcasted_iota(jnp.int32, (128, nq), 1)
    e_static = jnp.where(jnp.logical_and(ext_row >= 64, ext_row < 72), 1.0,
                         jnp.where(jnp.logical_and(ext_row >= 72, ext_row < 80),
                                   key_col.astype(F32), 0.0))
    ones_blk = jnp.ones((nq, 128), BF16)
    rr = lax.broadcasted_iota(jnp.int32, (2 * nq, nq), 0) % nq
    cc = lax.broadcasted_iota(jnp.int32, (2 * nq, nq), 1)
    causal_ok = cc <= rr

    n_dyn = N_BLOCKS // 2
    visits = []
    for v in range(N_VISITS):
        is_lo = v <= t
        tile = jnp.where(is_lo, 0, 1) if v < n_dyn else 1
        j = jnp.where(is_lo, v, v - t - 1)
        start = pl.multiple_of(j * nq, nq)
        e_j = jnp.where(jnp.logical_and(ext_row < 64, ext_row // 8 == j), 1.0,
                        e_static).astype(BF16)
        visits.append((is_lo, tile, start, e_j, j))

    def score_visit(p, v):
        _, tile, _, e_j, j = visits[v]
        k_aug_t = jnp.concatenate([kt_ref[0, j, p * 128:(p + 1) * 128, :], e_j], axis=0)
        s = _dot(qaug_ref[tile, p], k_aug_t)
        if v == N_VISITS - 1:
            s = jnp.where(causal_ok, s, NEG)
        elif v < n_dyn:
            s = jnp.where(jnp.logical_or(causal_ok, v != t), s, NEG)
        s_ref[p % 2, v] = s
        mxv_ref[p % 2, v] = jnp.maximum(s[:, :128], s[:, 128:])

    def row_max(p):
        buf = p % 2
        m_hi = mxv_ref[buf, n_dyn]
        for v in range(n_dyn + 1, N_VISITS):
            m_hi = jnp.maximum(m_hi, mxv_ref[buf, v])
        m_lo = jnp.full((2 * nq, 128), NEG, F32)
        for v in range(n_dyn):
            mv = mxv_ref[buf, v]
            m_lo = jnp.maximum(m_lo, jnp.where(visits[v][0], mv, NEG))
            m_hi = jnp.maximum(m_hi, jnp.where(visits[v][0], NEG, mv))
        m_ref[buf, 0] = jnp.broadcast_to(jnp.max(m_lo, axis=1, keepdims=True), (2 * nq, 128))
        m_ref[buf, 1] = jnp.broadcast_to(jnp.max(m_hi, axis=1, keepdims=True), (2 * nq, 128))

    def value_visit(p, v):
        buf = p % 2
        _, tile, start, _, _ = visits[v]
        cols = slice(p * 128, (p + 1) * 128)
        m = m_ref[buf, tile]
        pexp = jnp.exp(s_ref[buf, v] - jnp.concatenate([m, m], axis=1)).astype(BF16)
        v_ext = jnp.concatenate([v_ref[0, pl.ds(start, nq), cols], ones_blk], axis=1)
        r = _dot(pexp, v_ext)
        run = r if v == 0 else acc_ref[buf, 1] + r
        if v < n_dyn:
            last_lo = v == t
            if v == 0:
                acc_ref[buf, 0] = run
            else:
                acc_ref[buf, 0] = jnp.where(last_lo, run, acc_ref[buf, 0])
            run = jnp.where(last_lo, 0.0, run)
        acc_ref[buf, 1] = run

    def normalise(p):
        for tile in range(2):
            acc = acc_ref[p % 2, tile]
            o_full = acc[:, :128] / acc[:, 128:]
            pair = jnp.where(lane128 < HEAD_DIM, o_full[:nq], o_full[nq:])
            attn_ref[tile * nq:(tile + 1) * nq, p * 128:(p + 1) * 128] = pair.astype(BF16)

    for stage in range(N_PAIRS + 1):
        for v in range(N_VISITS):
            if stage < N_PAIRS:
                score_visit(stage, v)
            if stage > 0:
                value_visit(stage - 1, v)
        if stage > 0:
            normalise(stage - 1)
        if stage < N_PAIRS:
            row_max(stage)
    y_attn = _dot(attn_ref[...], wba_ref[...])

    merged = gates_ref[:, :D_MODEL] * y_pool + gates_ref[:, D_MODEL:] * y_attn
    out = h1 + _dot(merged.astype(BF16), wout_ref[...])
    olo_ref[0] = out[:nq]
    ohi_ref[0] = out[nq:]


def _mixer(h1, p, q, k, v, kmean, g_mix, w_gate, pool_w, pool_scale, w_bp, w_ba, w_out):
    b, s, d = h1.shape
    nq = MOBA_BLOCK
    nt = s // nq
    assert nt == N_BLOCKS
    halo_per_tile = nq // MAX_POOL_HALO
    lo = lambda bi, t: (bi, t, 0)
    hi = lambda bi, t: (bi, nt - 1 - t, 0)
    halo_lo = lambda bi, t: (bi, jnp.maximum(t * halo_per_tile - 1, 0), 0)
    halo_hi = lambda bi, t: (bi, (nt - 1 - t) * halo_per_tile - 1, 0)
    seq = lambda bi, t: (bi, 0, 0)
    c2 = lambda bi, t: (0, 0)
    c3 = lambda bi, t: (0, 0, 0)
    one = pl.Buffered(1)
    half = jax.ShapeDtypeStruct((b, s // 2, d), F32)
    return pl.pallas_call(
        _mixer_kernel,
        grid=(b, nt // 2),
        in_specs=[
            pl.BlockSpec((1, nq, d), lo),
            pl.BlockSpec((1, nq, d), hi),
            pl.BlockSpec((1, nq, POOL_WIDTH), lo),
            pl.BlockSpec((1, nq, POOL_WIDTH), hi),
            pl.BlockSpec((1, MAX_POOL_HALO, POOL_WIDTH), halo_lo),
            pl.BlockSpec((1, MAX_POOL_HALO, POOL_WIDTH), halo_hi),
            pl.BlockSpec((1, nq, ATTN_WIDTH), lo),
            pl.BlockSpec((1, nq, ATTN_WIDTH), hi),
            pl.BlockSpec((1, nt, ATTN_WIDTH, nq), lambda bi, t: (bi, 0, 0, 0)),
            pl.BlockSpec((1, s, ATTN_WIDTH), seq),
            pl.BlockSpec((1, nt, ATTN_WIDTH), seq),
            pl.BlockSpec((1, d), c2),
            pl.BlockSpec((d, 2 * d), c2, pipeline_mode=one),
            pl.BlockSpec((N_POOL_GROUPS, POOL_GROUP_DIM, POOL_GROUP_DIM), c3, pipeline_mode=one),
            pl.BlockSpec((1, POOL_WIDTH), c2),
            pl.BlockSpec((POOL_WIDTH, d), c2, pipeline_mode=one),
            pl.BlockSpec((ATTN_WIDTH, d), c2, pipeline_mode=one),
            pl.BlockSpec((d, d), c2, pipeline_mode=one),
        ],
        out_specs=[pl.BlockSpec((1, nq, d), lo), pl.BlockSpec((1, nq, d), lo)],
        out_shape=[half, half],
        scratch_shapes=[
            pltpu.VMEM((MAX_POOL_HALO + nq, POOL_WIDTH), F32),
            pltpu.VMEM((2, N_PAIRS, 2 * nq, 256), BF16),
            pltpu.VMEM((2, N_VISITS, 2 * nq, 128), F32),
            pltpu.VMEM((2, 2, 2 * nq, 128), F32),
            pltpu.VMEM((2, N_VISITS, 2 * nq, nq), F32),
            pltpu.VMEM((2, 2, 2 * nq, 256), F32),
            pltpu.VMEM((2 * nq, ATTN_WIDTH), BF16),
            pltpu.VMEM((2 * nq, POOL_WIDTH), BF16),
            pltpu.VMEM((2 * nq, d), BF16),
            pltpu.VMEM((2 * nq, 2 * d), F32),
        ],
        compiler_params=pltpu.CompilerParams(
            dimension_semantics=("arbitrary", "arbitrary"),
            vmem_limit_bytes=VMEM_LIMIT_BYTES),
        name="mixer",
    )(h1, h1, p, p, p, p, q, q, k, v, kmean, g_mix, w_gate, pool_w, pool_scale,
      w_bp, w_ba, w_out)


def kernel(x, ffn1_norm, ffn1_w_gate, ffn1_w_up, ffn1_w_down, mix_norm, w_in,
           pool_w, pool_scale, w_branch_pool, w_branch_attn, w_out,
           ffn2_norm, ffn2_w_gate, ffn2_w_up, ffn2_w_down, final_norm):
    b, s, d = x.shape
    assert d == D_MODEL and s == N_BLOCKS * MOBA_BLOCK
    depth = ffn1_norm.shape[0]
    bf = lambda a: a.astype(BF16)
    h = x
    for l in range(depth):
        last = l == depth - 1
        h = _ffn(h.reshape(b * s, d), ffn1_norm[l][None], bf(ffn1_w_gate[l]),
                 bf(ffn1_w_up[l]), bf(ffn1_w_down[l]))
        w_l = bf(w_in[l])
        p, q, kt, v, kmean = _inproj(h, mix_norm[l][None], w_l[:, :QKV_WIDTH])
        shp = lambda a: a.reshape(b, s, a.shape[-1])
        h_lo, h_hi = _mixer(h.reshape(b, s, d), shp(p), shp(q),
                            kt.reshape(b, N_BLOCKS, ATTN_WIDTH, MOBA_BLOCK), shp(v),
                            kmean.reshape(b, N_BLOCKS, ATTN_WIDTH),
                            mix_norm[l][None], w_l[:, QKV_WIDTH:], bf(pool_w[l]),
                            pool_scale[l][None], bf(w_branch_pool[l]),
                            bf(w_branch_attn[l]), bf(w_out[l]))
        h = _ffn_paired(h_lo, h_hi, ffn2_norm[l][None], bf(ffn2_w_gate[l]),
                        bf(ffn2_w_up[l]), bf(ffn2_w_down[l]), final_norm[None], last)
    return h
```

```python
import functools

import jax
import jax.numpy as jnp
from jax import lax
from jax.experimental import pallas as pl
from jax.experimental.pallas import tpu as pltpu

EPS = 1e-6
D_MODEL = 1024
D_FF = 2816
POOL_WIDTH = 512
N_POOL_GROUPS = 4
POOL_GROUP_DIM = 128
POOL_WINDOWS = (2, 4, 8, 16)
ATTN_WIDTH = 512
N_HEADS = 8
HEAD_DIM = 64
N_PAIRS = N_HEADS // 2
MOBA_BLOCK = 256
MOBA_TOPK = 3
N_BLOCKS = 8
N_VISITS = N_BLOCKS + 1
PAIR_GROUP = 2
QKV_WIDTH = POOL_WIDTH + 3 * ATTN_WIDTH
NEG = -1e30

VMEM_LIMIT_BYTES = 58 * 1024 * 1024
FFN_TILE = N_BLOCKS * MOBA_BLOCK
FFN_SUB = 512
FFN_CHUNKS = ((0, 1024), (1024, 1024), (2048, 768))
INPROJ_TILE = 1024
MAX_POOL_HALO = 16
GATE_CHUNK = 2 * D_MODEL // (2 * N_POOL_GROUPS)

BF16 = jnp.bfloat16
F32 = jnp.float32


def _rms_norm(x, g):
    ms = jnp.mean(x * x, axis=-1, keepdims=True)
    return x * lax.rsqrt(ms + EPS) * g


def _dot(a, b):
    return jnp.dot(a, b, preferred_element_type=F32)


def _dot_nt(a, b):
    return lax.dot_general(a, b, (((1,), (1,)), ((), ())),
                           preferred_element_type=F32)


def _ffn_rows(xs, g_ref, wg_ref, wu_ref, wd_ref, gf_ref, store):
    def norm(x):
        return _rms_norm(x, g_ref[...]).astype(BF16)

    def finish(n, acc):
        h = xs[n] + 0.5 * acc
        if gf_ref is not None:
            h = _rms_norm(h, gf_ref[...])
        store(n, h)

    xns = {0: norm(xs[0])}

    def gate_up(n, ci):
        start, size = FFN_CHUNKS[ci]
        return (_dot(xns[n], wg_ref[:, start:start + size]),
                _dot(xns[n], wu_ref[:, start:start + size]))

    items = [(n, ci) for n in range(len(xs)) for ci in range(len(FFN_CHUNKS))]
    ab = gate_up(*items[0])
    acc = None
    pending = None
    for k, (n, ci) in enumerate(items):
        if ci == 0 and n + 1 < len(xs):
            xns[n + 1] = norm(xs[n + 1])
        ab_next = gate_up(*items[k + 1]) if k + 1 < len(items) else None
        if ci == 0 and pending is not None:
            finish(*pending)
        a, b = ab
        start, size = FFN_CHUNKS[ci]
        hmid = (a * jax.nn.sigmoid(a) * b).astype(BF16)
        d = _dot(hmid, wd_ref[start:start + size, :])
        acc = d if ci == 0 else acc + d
        if ci == len(FFN_CHUNKS) - 1:
            pending = (n, acc)
        ab = ab_next
    finish(*pending)


def _ffn_kernel(x_ref, g_ref, wg_ref, wu_ref, wd_ref, o_ref):
    groups = [slice(n * FFN_SUB, (n + 1) * FFN_SUB) for n in range(FFN_TILE // FFN_SUB)]

    def store(n, h):
        o_ref[groups[n], :] = h

    _ffn_rows([x_ref[rows, :] for rows in groups], g_ref, wg_ref, wu_ref, wd_ref, None, store)


def _ffn_weight_specs(d, const):
    one = pl.Buffered(1)
    return [
        pl.BlockSpec((1, d), const),
        pl.BlockSpec((d, D_FF), const, pipeline_mode=one),
        pl.BlockSpec((d, D_FF), const, pipeline_mode=one),
        pl.BlockSpec((D_FF, d), const, pipeline_mode=one),
    ]


def _ffn(x2d, g, wg, wu, wd):
    t, d = x2d.shape
    return pl.pallas_call(
        _ffn_kernel,
        grid=(t // FFN_TILE,),
        in_specs=[pl.BlockSpec((FFN_TILE, d), lambda i: (i, 0))]
        + _ffn_weight_specs(d, lambda i: (0, 0)),
        out_specs=pl.BlockSpec((FFN_TILE, d), lambda i: (i, 0)),
        out_shape=jax.ShapeDtypeStruct((t, d), F32),
        compiler_params=pltpu.CompilerParams(
            dimension_semantics=("arbitrary",),
            vmem_limit_bytes=VMEM_LIMIT_BYTES),
        name="ffn",
    )(x2d, g, wg, wu, wd)


def _ffn_paired_kernel(lo_ref, hi_ref, g_ref, wg_ref, wu_ref, wd_ref, gf_ref, o_ref,
                       *, final_norm):
    half_s = lo_ref.shape[1]
    n_hi_tiles = half_s // MOBA_BLOCK
    per_group = FFN_SUB // MOBA_BLOCK
    xs = [lo_ref[0, n * FFN_SUB:(n + 1) * FFN_SUB, :] for n in range(half_s // FFN_SUB)]
    for n in range(half_s // FFN_SUB):
        tiles = [n_hi_tiles - 1 - (n * per_group + k) for k in range(per_group)]
        xs.append(jnp.concatenate(
            [hi_ref[0, i * MOBA_BLOCK:(i + 1) * MOBA_BLOCK, :] for i in tiles], axis=0))

    def store(n, h):
        o_ref[0, n * FFN_SUB:(n + 1) * FFN_SUB, :] = h

    _ffn_rows(xs, g_ref, wg_ref, wu_ref, wd_ref, gf_ref if final_norm else None, store)


def _ffn_paired(h_lo, h_hi, g, wg, wu, wd, gf, final_norm):
    b, half_s, d = h_lo.shape
    assert 2 * half_s == FFN_TILE and half_s % FFN_SUB == 0
    const = lambda bi: (0, 0)
    seq = lambda bi: (bi, 0, 0)
    return pl.pallas_call(
        functools.partial(_ffn_paired_kernel, final_norm=final_norm),
        grid=(b,),
        in_specs=[pl.BlockSpec((1, half_s, d), seq), pl.BlockSpec((1, half_s, d), seq)]
        + _ffn_weight_specs(d, const) + [pl.BlockSpec((1, d), const)],
        out_specs=pl.BlockSpec((1, FFN_TILE, d), seq),
        out_shape=jax.ShapeDtypeStruct((b, FFN_TILE, d), F32),
        compiler_params=pltpu.CompilerParams(
            dimension_semantics=("arbitrary",),
            vmem_limit_bytes=VMEM_LIMIT_BYTES),
        name="ffn_final" if final_norm else "ffn_paired",
    )(h_lo, h_hi, g, wg, wu, wd, gf)


def _inproj_kernel(h_ref, g_ref, w_ref, p_ref, q_ref, kt_ref, v_ref, km_ref):
    u = _rms_norm(h_ref[...], g_ref[...]).astype(BF16)
    proj = _dot(u, w_ref[...])
    p_ref[...] = proj[:, :POOL_WIDTH]
    o = POOL_WIDTH
    q_ref[...] = (proj[:, o:o + ATTN_WIDTH] * (HEAD_DIM ** -0.5)).astype(BF16)
    k = proj[:, o + ATTN_WIDTH:o + 2 * ATTN_WIDTH]
    v_ref[...] = proj[:, o + 2 * ATTN_WIDTH:o + 3 * ATTN_WIDTH].astype(BF16)
    for blk in range(INPROJ_TILE // MOBA_BLOCK):
        kb = k[blk * MOBA_BLOCK:(blk + 1) * MOBA_BLOCK, :]
        kt_ref[blk] = kb.T.astype(BF16)
        km_ref[0, blk:blk + 1, :] = jnp.mean(kb, axis=0, keepdims=True)


def _inproj(h2d, g, w_qkv):
    t, d = h2d.shape
    nblk = INPROJ_TILE // MOBA_BLOCK
    row = lambda i: (i, 0)
    const = lambda i: (0, 0)
    return pl.pallas_call(
        _inproj_kernel,
        grid=(t // INPROJ_TILE,),
        in_specs=[
            pl.BlockSpec((INPROJ_TILE, d), row),
            pl.BlockSpec((1, d), const),
            pl.BlockSpec((d, QKV_WIDTH), const, pipeline_mode=pl.Buffered(1)),
        ],
        out_specs=[
            pl.BlockSpec((INPROJ_TILE, POOL_WIDTH), row),
            pl.BlockSpec((INPROJ_TILE, ATTN_WIDTH), row),
            pl.BlockSpec((nblk, ATTN_WIDTH, MOBA_BLOCK), lambda i: (i, 0, 0)),
            pl.BlockSpec((INPROJ_TILE, ATTN_WIDTH), row),
            pl.BlockSpec((1, nblk, ATTN_WIDTH), lambda i: (i, 0, 0)),
        ],
        out_shape=[
            jax.ShapeDtypeStruct((t, POOL_WIDTH), F32),
            jax.ShapeDtypeStruct((t, ATTN_WIDTH), BF16),
            jax.ShapeDtypeStruct((t // MOBA_BLOCK, ATTN_WIDTH, MOBA_BLOCK), BF16),
            jax.ShapeDtypeStruct((t, ATTN_WIDTH), BF16),
            jax.ShapeDtypeStruct((t // INPROJ_TILE, nblk, ATTN_WIDTH), F32),
        ],
        compiler_params=pltpu.CompilerParams(
            dimension_semantics=("arbitrary",),
            vmem_limit_bytes=VMEM_LIMIT_BYTES),
        name="inproj",
    )(h2d, g, w_qkv)


def _pool_branch(i, p_ref, halo_ref, poolw_ref, pscale_ref, pext_ref, ypool_ref, row0, tick):
    nq = MOBA_BLOCK
    halo = halo_ref[0]
    pext_ref[:MAX_POOL_HALO, :] = jnp.where(i > 0, halo, jnp.zeros_like(halo))
    pext_ref[MAX_POOL_HALO:, :] = p_ref[0]
    t_glob = i * nq + lax.broadcasted_iota(jnp.int32, (nq, POOL_GROUP_DIM), 0)
    for g, w in enumerate(POOL_WINDOWS):
        cols = slice(g * POOL_GROUP_DIM, (g + 1) * POOL_GROUP_DIM)
        xg = pext_ref[MAX_POOL_HALO:, cols]
        win = xg
        for s in range(1, w):
            win = win + pext_ref[MAX_POOL_HALO - s:MAX_POOL_HALO - s + nq, cols]
        count = jnp.minimum(t_glob + 1, w).astype(F32)
        mixed = (win / count - xg).astype(BF16)
        yg = _dot(mixed, poolw_ref[g]) * pscale_ref[:, cols]
        ypool_ref[row0:row0 + nq, cols] = yg.astype(BF16)
        tick()


def _augmented_queries(i, q, km_stack, qaug_ref, tile):
    nq = MOBA_BLOCK
    lane128 = lax.broadcasted_iota(jnp.int32, (nq, 128), 1)
    gt = _dot_nt(km_stack, q)
    gate = gt[:64, :] + gt[64:, :]
    g_n = [gate[n * 8:(n + 1) * 8, :] for n in range(N_BLOCKS)]
    slope = jnp.exp2(-(lax.broadcasted_iota(jnp.int32, (N_HEADS, nq), 0) + 1).astype(F32))
    r_lane = lax.broadcasted_iota(jnp.int32, (N_HEADS, nq), 1).astype(F32)
    pre = []
    for n in range(N_BLOCKS):
        cnt = jnp.zeros((N_HEADS, nq), F32)
        for n2 in range(N_BLOCKS):
            if n2 == n:
                continue
            beats = (g_n[n2] >= g_n[n]) if n2 < n else (g_n[n2] > g_n[n])
            cnt = cnt + jnp.where(jnp.logical_and(beats, n2 < i), 1.0, 0.0)
        sel = jnp.logical_and(cnt < MOBA_TOPK, n < i)
        dist = (i - n).astype(F32) * float(MOBA_BLOCK)
        bias = jnp.where(sel, -slope * dist, NEG)
        pre.append(jnp.where(n == i, 0.0, bias))
    pre.append(-slope * r_lane)
    pre.append(slope)
    pre.append(jnp.zeros((128 - 80, nq), F32))
    q_ext_all = jnp.concatenate(pre, axis=0).T
    for p in range(N_PAIRS):
        qp = q[:, p * 128:(p + 1) * 128].astype(F32)
        for half in range(2):
            h = 2 * p + half
            in_head = (lane128 >= HEAD_DIM) if half else (lane128 < HEAD_DIM)
            qh = jnp.where(in_head, qp, 0.0).astype(BF16)
            ext_lane = jnp.logical_or(
                jnp.logical_and(lane128 < 64, lane128 % 8 == h),
                jnp.logical_or(lane128 == 64 + h, lane128 == 72 + h))
            qe = jnp.where(ext_lane, q_ext_all, 0.0).astype(BF16)
            rows = slice(half * nq, (half + 1) * nq)
            qaug_ref[tile, p, rows, :128] = qh
            qaug_ref[tile, p, rows, 128:] = qe


def _mixer_kernel(hlo_ref, hhi_ref, plo_ref, phi_ref, halo_lo_ref, halo_hi_ref,
                  qlo_ref, qhi_ref, kt_ref, v_ref, km_ref,
                  gmix_ref, wgate_ref, poolw_ref, pscale_ref, wbp_ref, wba_ref, wout_ref,
                  olo_ref, ohi_ref,
                  pext_ref, qaug_ref, mxv_ref, m_ref, s_ref, acc_ref, attn_ref, ypool_ref,
                  u_ref, gates_ref):
    t = pl.program_id(1)
    nq = MOBA_BLOCK
    i_lo, i_hi = t, N_BLOCKS - 1 - t
    lane128 = lax.broadcasted_iota(jnp.int32, (nq, 128), 1)

    h1 = jnp.concatenate([hlo_ref[0], hhi_ref[0]], axis=0)
    u_ref[...] = _rms_norm(h1, gmix_ref[...]).astype(BF16)
    gate_chunks = iter(range(0, 2 * D_MODEL, GATE_CHUNK))

    def tick():
        c = next(gate_chunks)
        z = _dot(u_ref[...], wgate_ref[:, c:c + GATE_CHUNK])
        gates_ref[:, c:c + GATE_CHUNK] = 0.5 * jnp.tanh(0.5 * z) + 0.5

    _pool_branch(i_lo, plo_ref, halo_lo_ref, poolw_ref, pscale_ref, pext_ref, ypool_ref, 0, tick)
    _pool_branch(i_hi, phi_ref, halo_hi_ref, poolw_ref, pscale_ref, pext_ref, ypool_ref, nq, tick)
    y_pool = _dot(ypool_ref[...], wbp_ref[...])

    km = km_ref[0]
    head_of_lane = lax.broadcasted_iota(jnp.int32, (N_HEADS, ATTN_WIDTH), 1) // HEAD_DIM
    head_of_row = lax.broadcasted_iota(jnp.int32, (N_HEADS, ATTN_WIDTH), 0)
    head_mask = head_of_lane == head_of_row
    slabs = [jnp.where(head_mask, jnp.broadcast_to(km[n:n + 1, :], (N_HEADS, ATTN_WIDTH)), 0.0)
             for n in range(N_BLOCKS)]
    km_bd = jnp.concatenate(slabs, axis=0)
    km_hi = km_bd.astype(BF16)
    km_lo = (km_bd - km_hi.astype(F32)).astype(BF16)
    km_stack = jnp.concatenate([km_hi, km_lo], axis=0)
    _augmented_queries(i_lo, qlo_ref[0], km_stack, qaug_ref, 0)
    _augmented_queries(i_hi, qhi_ref[0], km_stack, qaug_ref, 1)

    ext_row = lax.broadcasted_iota(jnp.int32, (128, nq), 0)
    key_col = lax.broadcasted_iota(jnp.int32, (128, nq), 1)
    e_static = jnp.where(jnp.logical_and(ext_row >= 64, ext_row < 72), 1.0,
                         jnp.where(jnp.logical_and(ext_row >= 72, ext_row < 80),
                                   key_col.astype(F32), 0.0))
    ones_blk = jnp.ones((nq, 128), BF16)
    rr = lax.broadcasted_iota(jnp.int32, (2 * nq, nq), 0) % nq
    cc = lax.broadcasted_iota(jnp.int32, (2 * nq, nq), 1)
    causal_ok = cc <= rr

    n_dyn = N_BLOCKS // 2
    visits = []
    for v in range(N_VISITS):
        is_lo = v <= t
        tile = jnp.where(is_lo, 0, 1) if v < n_dyn else 1
        j = jnp.where(is_lo, v, v - t - 1)
        start = pl.multiple_of(j * nq, nq)
        e_j = jnp.where(jnp.logical_and(ext_row < 64, ext_row // 8 == j), 1.0,
                        e_static).astype(BF16)
        visits.append((is_lo, tile, start, e_j, j))

    def slot(p):
        return ((p // PAIR_GROUP) % 2) * PAIR_GROUP + p % PAIR_GROUP

    def score_visit(p, v):
        _, tile, _, e_j, j = visits[v]
        k_aug_t = jnp.concatenate([kt_ref[0, j, p * 128:(p + 1) * 128, :], e_j], axis=0)
        s = _dot(qaug_ref[tile, p], k_aug_t)
        if v == N_VISITS - 1:
            s = jnp.where(causal_ok, s, NEG)
        elif v < n_dyn:
            s = jnp.where(jnp.logical_or(causal_ok, v != t), s, NEG)
        s_ref[slot(p), v] = s
        fold = jnp.maximum(s[:, :128], s[:, 128:])
        if v <= n_dyn:
            mxv_ref[slot(p), v] = fold
        else:
            mxv_ref[slot(p), n_dyn] = jnp.maximum(mxv_ref[slot(p), n_dyn], fold)

    def row_max(p):
        buf = slot(p)
        m_hi = mxv_ref[buf, n_dyn]
        m_lo = jnp.full((2 * nq, 128), NEG, F32)
        for v in range(n_dyn):
            mv = mxv_ref[buf, v]
            m_lo = jnp.maximum(m_lo, jnp.where(visits[v][0], mv, NEG))
            m_hi = jnp.maximum(m_hi, jnp.where(visits[v][0], NEG, mv))
        m_ref[buf, 0] = jnp.broadcast_to(jnp.max(m_lo, axis=1, keepdims=True), (2 * nq, 128))
        m_ref[buf, 1] = jnp.broadcast_to(jnp.max(m_hi, axis=1, keepdims=True), (2 * nq, 128))

    def value_visit(p, v):
        buf = slot(p)
        _, tile, start, _, _ = visits[v]
        cols = slice(p * 128, (p + 1) * 128)
        m = m_ref[buf, tile]
        pexp = jnp.exp(s_ref[buf, v] - jnp.concatenate([m, m], axis=1)).astype(BF16)
        v_ext = jnp.concatenate([v_ref[0, pl.ds(start, nq), cols], ones_blk], axis=1)
        r = _dot(pexp, v_ext)
        run = r if v == 0 else acc_ref[buf, 1] + r
        if v < n_dyn:
            last_lo = v == t
            if v == 0:
                acc_ref[buf, 0] = run
            else:
                acc_ref[buf, 0] = jnp.where(last_lo, run, acc_ref[buf, 0])
            run = jnp.where(last_lo, 0.0, run)
        acc_ref[buf, 1] = run

    def normalise(p):
        for tile in range(2):
            acc = acc_ref[slot(p), tile]
            o_full = acc[:, :128] / acc[:, 128:]
            pair = jnp.where(lane128 < HEAD_DIM, o_full[:nq], o_full[nq:])
            attn_ref[tile * nq:(tile + 1) * nq, p * 128:(p + 1) * 128] = pair.astype(BF16)

    n_stages = N_PAIRS // PAIR_GROUP
    for stage in range(n_stages + 1):
        now = [stage * PAIR_GROUP + g for g in range(PAIR_GROUP)] if stage < n_stages else []
        before = [(stage - 1) * PAIR_GROUP + g for g in range(PAIR_GROUP)] if stage > 0 else []
        for v in range(N_VISITS):
            for p in now:
                score_visit(p, v)
            for p in before:
                value_visit(p, v)
        for p in before:
            normalise(p)
        for p in now:
            row_max(p)
    y_attn = _dot(attn_ref[...], wba_ref[...])

    merged = gates_ref[:, :D_MODEL] * y_pool + gates_ref[:, D_MODEL:] * y_attn
    out = h1 + _dot(merged.astype(BF16), wout_ref[...])
    olo_ref[0] = out[:nq]
    ohi_ref[0] = out[nq:]


def _mixer(h1, p, q, k, v, kmean, g_mix, w_gate, pool_w, pool_scale, w_bp, w_ba, w_out):
    b, s, d = h1.shape
    nq = MOBA_BLOCK
    nt = s // nq
    assert nt == N_BLOCKS
    halo_per_tile = nq // MAX_POOL_HALO
    lo = lambda bi, t: (bi, t, 0)
    hi = lambda bi, t: (bi, nt - 1 - t, 0)
    halo_lo = lambda bi, t: (bi, jnp.maximum(t * halo_per_tile - 1, 0), 0)
    halo_hi = lambda bi, t: (bi, (nt - 1 - t) * halo_per_tile - 1, 0)
    seq = lambda bi, t: (bi, 0, 0)
    c2 = lambda bi, t: (0, 0)
    c3 = lambda bi, t: (0, 0, 0)
    one = pl.Buffered(1)
    half = jax.ShapeDtypeStruct((b, s // 2, d), F32)
    return pl.pallas_call(
        _mixer_kernel,
        grid=(b, nt // 2),
        in_specs=[
            pl.BlockSpec((1, nq, d), lo),
            pl.BlockSpec((1, nq, d), hi),
            pl.BlockSpec((1, nq, POOL_WIDTH), lo),
            pl.BlockSpec((1, nq, POOL_WIDTH), hi),
            pl.BlockSpec((1, MAX_POOL_HALO, POOL_WIDTH), halo_lo),
            pl.BlockSpec((1, MAX_POOL_HALO, POOL_WIDTH), halo_hi),
            pl.BlockSpec((1, nq, ATTN_WIDTH), lo),
            pl.BlockSpec((1, nq, ATTN_WIDTH), hi),
            pl.BlockSpec((1, nt, ATTN_WIDTH, nq), lambda bi, t: (bi, 0, 0, 0)),
            pl.BlockSpec((1, s, ATTN_WIDTH), seq),
            pl.BlockSpec((1, nt, ATTN_WIDTH), seq),
            pl.BlockSpec((1, d), c2),
            pl.BlockSpec((d, 2 * d), c2, pipeline_mode=one),
            pl.BlockSpec((N_POOL_GROUPS, POOL_GROUP_DIM, POOL_GROUP_DIM), c3, pipeline_mode=one),
            pl.BlockSpec((1, POOL_WIDTH), c2),
            pl.BlockSpec((POOL_WIDTH, d), c2, pipeline_mode=one),
            pl.BlockSpec((ATTN_WIDTH, d), c2, pipeline_mode=one),
            pl.BlockSpec((d, d), c2, pipeline_mode=one),
        ],
        out_specs=[pl.BlockSpec((1, nq, d), lo), pl.BlockSpec((1, nq, d), lo)],
        out_shape=[half, half],
        scratch_shapes=[
            pltpu.VMEM((MAX_POOL_HALO + nq, POOL_WIDTH), F32),
            pltpu.VMEM((2, N_PAIRS, 2 * nq, 256), BF16),
            pltpu.VMEM((2 * PAIR_GROUP, N_BLOCKS // 2 + 1, 2 * nq, 128), F32),
            pltpu.VMEM((2 * PAIR_GROUP, 2, 2 * nq, 128), F32),
            pltpu.VMEM((2 * PAIR_GROUP, N_VISITS, 2 * nq, nq), F32),
            pltpu.VMEM((2 * PAIR_GROUP, 2, 2 * nq, 256), F32),
            pltpu.VMEM((2 * nq, ATTN_WIDTH), BF16),
            pltpu.VMEM((2 * nq, POOL_WIDTH), BF16),
            pltpu.VMEM((2 * nq, d), BF16),
            pltpu.VMEM((2 * nq, 2 * d), F32),
        ],
        compiler_params=pltpu.CompilerParams(
            dimension_semantics=("arbitrary", "arbitrary"),
            vmem_limit_bytes=VMEM_LIMIT_BYTES),
        name="mixer",
    )(h1, h1, p, p, p, p, q, q, k, v, kmean, g_mix, w_gate, pool_w, pool_scale,
      w_bp, w_ba, w_out)


def kernel(x, ffn1_norm, ffn1_w_gate, ffn1_w_up, ffn1_w_down, mix_norm, w_in,
           pool_w, pool_scale, w_branch_pool, w_branch_attn, w_out,
           ffn2_norm, ffn2_w_gate, ffn2_w_up, ffn2_w_down, final_norm):
    b, s, d = x.shape
    assert d == D_MODEL and s == N_BLOCKS * MOBA_BLOCK
    depth = ffn1_norm.shape[0]
    bf = lambda a: a.astype(BF16)
    h = x
    for l in range(depth):
        last = l == depth - 1
        h = _ffn(h.reshape(b * s, d), ffn1_norm[l][None], bf(ffn1_w_gate[l]),
                 bf(ffn1_w_up[l]), bf(ffn1_w_down[l]))
        w_l = bf(w_in[l])
        p, q, kt, v, kmean = _inproj(h, mix_norm[l][None], w_l[:, :QKV_WIDTH])
        shp = lambda a: a.reshape(b, s, a.shape[-1])
        h_lo, h_hi = _mixer(h.reshape(b, s, d), shp(p), shp(q),
                            kt.reshape(b, N_BLOCKS, ATTN_WIDTH, MOBA_BLOCK), shp(v),
                            kmean.reshape(b, N_BLOCKS, ATTN_WIDTH),
                            mix_norm[l][None], w_l[:, QKV_WIDTH:], bf(pool_w[l]),
                            pool_scale[l][None], bf(w_branch_pool[l]),
                            bf(w_branch_attn[l]), bf(w_out[l]))
        h = _ffn_paired(h_lo, h_hi, ffn2_norm[l][None], bf(ffn2_w_gate[l]),
                        bf(ffn2_w_up[l]), bf(ffn2_w_down[l]), final_norm[None], last)
    return h
```

```python
import functools

import jax
import jax.numpy as jnp
from jax import lax
from jax.experimental import pallas as pl
from jax.experimental.pallas import tpu as pltpu

EPS = 1e-6
D_MODEL = 1024
D_FF = 2816
POOL_WIDTH = 512
N_POOL_GROUPS = 4
POOL_GROUP_DIM = 128
POOL_WINDOWS = (2, 4, 8, 16)
ATTN_WIDTH = 512
N_HEADS = 8
HEAD_DIM = 64
N_PAIRS = N_HEADS // 2
MOBA_BLOCK = 256
MOBA_TOPK = 3
N_BLOCKS = 8
N_VISITS = N_BLOCKS + 1
PAIR_GROUP = 2
QKV_WIDTH = POOL_WIDTH + 3 * ATTN_WIDTH
NEG = -1e30

VMEM_LIMIT_BYTES = 58 * 1024 * 1024
FFN_TILE = N_BLOCKS * MOBA_BLOCK
FFN_SUB = 512
FFN_CHUNKS = ((0, 1024), (1024, 1024), (2048, 768))
W_STAGE_ROWS_IN = 32
W_STAGE_ROWS_OUT = 176
INPROJ_TILE = 1024
MAX_POOL_HALO = 16
GATE_CHUNK = 2 * D_MODEL // (2 * N_POOL_GROUPS)

BF16 = jnp.bfloat16
F32 = jnp.float32


def _rms_norm(x, g):
    ms = jnp.mean(x * x, axis=-1, keepdims=True)
    return x * lax.rsqrt(ms + EPS) * g


def _dot(a, b):
    return jnp.dot(a, b, preferred_element_type=F32)


def _dot_nt(a, b):
    return lax.dot_general(a, b, (((1,), (1,)), ((), ())),
                           preferred_element_type=F32)


def _ffn_rows(xs, g_ref, wg_ref, wu_ref, wd_ref, gf_ref, store):
    def norm(x):
        return _rms_norm(x, g_ref[...]).astype(BF16)

    def finish(n, acc):
        h = xs[n] + 0.5 * acc
        if gf_ref is not None:
            h = _rms_norm(h, gf_ref[...])
        store(n, h)

    xns = {0: norm(xs[0])}

    def gate_up(n, ci):
        start, size = FFN_CHUNKS[ci]
        return (_dot(xns[n], wg_ref[:, start:start + size]),
                _dot(xns[n], wu_ref[:, start:start + size]))

    items = [(n, ci) for n in range(len(xs)) for ci in range(len(FFN_CHUNKS))]
    ab = gate_up(*items[0])
    acc = None
    pending = None
    for k, (n, ci) in enumerate(items):
        if ci == 0 and n + 1 < len(xs):
            xns[n + 1] = norm(xs[n + 1])
        ab_next = gate_up(*items[k + 1]) if k + 1 < len(items) else None
        if ci == 0 and pending is not None:
            finish(*pending)
        a, b = ab
        start, size = FFN_CHUNKS[ci]
        hmid = (a * jax.nn.sigmoid(a) * b).astype(BF16)
        d = _dot(hmid, wd_ref[start:start + size, :])
        acc = d if ci == 0 else acc + d
        if ci == len(FFN_CHUNKS) - 1:
            pending = (n, acc)
        ab = ab_next
    finish(*pending)


def _load_ffn_weights(w_hbm, w_vmem, stages, sem):
    for src, dst, stage in zip(w_hbm, w_vmem, stages):
        rows = stage.shape[1]
        n_chunks = src.shape[0] // rows

        def chunk_copy(k, slot, src=src, stage=stage, rows=rows):
            start = pl.multiple_of(k * rows, rows)
            return pltpu.make_async_copy(src.at[pl.ds(start, rows), :], stage.at[slot],
                                         sem.at[slot])

        chunk_copy(0, 0).start()

        def body(k, carry, dst=dst, stage=stage, rows=rows, n_chunks=n_chunks,
                 chunk_copy=chunk_copy):
            slot = k % 2

            @pl.when(k + 1 < n_chunks)
            def _():
                chunk_copy(k + 1, 1 - slot).start()

            chunk_copy(k, slot).wait()
            dst[pl.ds(pl.multiple_of(k * rows, rows), rows), :] = stage[slot].astype(BF16)
            return carry

        lax.fori_loop(0, n_chunks, body, 0)


def _ffn_kernel(x_ref, g_ref, wg_hbm, wu_hbm, wd_hbm, o_ref,
                wg_ref, wu_ref, wd_ref, stage_in, stage_out, sem):
    @pl.when(pl.program_id(0) == 0)
    def _():
        _load_ffn_weights((wg_hbm, wu_hbm, wd_hbm), (wg_ref, wu_ref, wd_ref),
                          (stage_in, stage_in, stage_out), sem)

    groups = [slice(n * FFN_SUB, (n + 1) * FFN_SUB) for n in range(FFN_TILE // FFN_SUB)]

    def store(n, h):
        o_ref[groups[n], :] = h

    _ffn_rows([x_ref[rows, :] for rows in groups], g_ref, wg_ref, wu_ref, wd_ref, None, store)


def _ffn_weight_specs(d, const):
    hbm = pl.BlockSpec(memory_space=pl.ANY)
    return [pl.BlockSpec((1, d), const), hbm, hbm, hbm]


def _ffn_weight_scratch(d):
    return [
        pltpu.VMEM((d, D_FF), BF16),
        pltpu.VMEM((d, D_FF), BF16),
        pltpu.VMEM((D_FF, d), BF16),
        pltpu.VMEM((2, W_STAGE_ROWS_IN, D_FF), F32),
        pltpu.VMEM((2, W_STAGE_ROWS_OUT, d), F32),
        pltpu.SemaphoreType.DMA((2,)),
    ]


def _ffn(x2d, g, wg, wu, wd):
    t, d = x2d.shape
    return pl.pallas_call(
        _ffn_kernel,
        grid=(t // FFN_TILE,),
        in_specs=[pl.BlockSpec((FFN_TILE, d), lambda i: (i, 0))]
        + _ffn_weight_specs(d, lambda i: (0, 0)),
        out_specs=pl.BlockSpec((FFN_TILE, d), lambda i: (i, 0)),
        out_shape=jax.ShapeDtypeStruct((t, d), F32),
        scratch_shapes=_ffn_weight_scratch(d),
        compiler_params=pltpu.CompilerParams(
            dimension_semantics=("arbitrary",),
            vmem_limit_bytes=VMEM_LIMIT_BYTES),
        name="ffn",
    )(x2d, g, wg, wu, wd)


def _ffn_paired_kernel(lo_ref, hi_ref, g_ref, wg_hbm, wu_hbm, wd_hbm, gf_ref, o_ref,
                       wg_ref, wu_ref, wd_ref, stage_in, stage_out, sem, *, final_norm):
    @pl.when(pl.program_id(0) == 0)
    def _():
        _load_ffn_weights((wg_hbm, wu_hbm, wd_hbm), (wg_ref, wu_ref, wd_ref),
                          (stage_in, stage_in, stage_out), sem)

    half_s = lo_ref.shape[1]
    n_hi_tiles = half_s // MOBA_BLOCK
    per_group = FFN_SUB // MOBA_BLOCK
    xs = [lo_ref[0, n * FFN_SUB:(n + 1) * FFN_SUB, :] for n in range(half_s // FFN_SUB)]
    for n in range(half_s // FFN_SUB):
        tiles = [n_hi_tiles - 1 - (n * per_group + k) for k in range(per_group)]
        xs.append(jnp.concatenate(
            [hi_ref[0, i * MOBA_BLOCK:(i + 1) * MOBA_BLOCK, :] for i in tiles], axis=0))

    def store(n, h):
        o_ref[0, n * FFN_SUB:(n + 1) * FFN_SUB, :] = h

    _ffn_rows(xs, g_ref, wg_ref, wu_ref, wd_ref, gf_ref if final_norm else None, store)


def _ffn_paired(h_lo, h_hi, g, wg, wu, wd, gf, final_norm):
    b, half_s, d = h_lo.shape
    assert 2 * half_s == FFN_TILE and half_s % FFN_SUB == 0
    const = lambda bi: (0, 0)
    seq = lambda bi: (bi, 0, 0)
    return pl.pallas_call(
        functools.partial(_ffn_paired_kernel, final_norm=final_norm),
        grid=(b,),
        in_specs=[pl.BlockSpec((1, half_s, d), seq), pl.BlockSpec((1, half_s, d), seq)]
        + _ffn_weight_specs(d, const) + [pl.BlockSpec((1, d), const)],
        out_specs=pl.BlockSpec((1, FFN_TILE, d), seq),
        out_shape=jax.ShapeDtypeStruct((b, FFN_TILE, d), F32),
        scratch_shapes=_ffn_weight_scratch(d),
        compiler_params=pltpu.CompilerParams(
            dimension_semantics=("arbitrary",),
            vmem_limit_bytes=VMEM_LIMIT_BYTES),
        name="ffn_final" if final_norm else "ffn_paired",
    )(h_lo, h_hi, g, wg, wu, wd, gf)


def _inproj_kernel(h_ref, g_ref, w_ref, p_ref, q_ref, kt_ref, v_ref, km_ref):
    u = _rms_norm(h_ref[...], g_ref[...]).astype(BF16)
    proj = _dot(u, w_ref[...])
    p_ref[...] = proj[:, :POOL_WIDTH]
    o = POOL_WIDTH
    q_ref[...] = (proj[:, o:o + ATTN_WIDTH] * (HEAD_DIM ** -0.5)).astype(BF16)
    k = proj[:, o + ATTN_WIDTH:o + 2 * ATTN_WIDTH]
    v_ref[...] = proj[:, o + 2 * ATTN_WIDTH:o + 3 * ATTN_WIDTH].astype(BF16)
    for blk in range(INPROJ_TILE // MOBA_BLOCK):
        kb = k[blk * MOBA_BLOCK:(blk + 1) * MOBA_BLOCK, :]
        kt_ref[blk] = kb.T.astype(BF16)
        km_ref[0, blk:blk + 1, :] = jnp.mean(kb, axis=0, keepdims=True)


def _inproj(h2d, g, w_qkv):
    t, d = h2d.shape
    nblk = INPROJ_TILE // MOBA_BLOCK
    row = lambda i: (i, 0)
    const = lambda i: (0, 0)
    return pl.pallas_call(
        _inproj_kernel,
        grid=(t // INPROJ_TILE,),
        in_specs=[
            pl.BlockSpec((INPROJ_TILE, d), row),
            pl.BlockSpec((1, d), const),
            pl.BlockSpec((d, QKV_WIDTH), const, pipeline_mode=pl.Buffered(1)),
        ],
        out_specs=[
            pl.BlockSpec((INPROJ_TILE, POOL_WIDTH), row),
            pl.BlockSpec((INPROJ_TILE, ATTN_WIDTH), row),
            pl.BlockSpec((nblk, ATTN_WIDTH, MOBA_BLOCK), lambda i: (i, 0, 0)),
            pl.BlockSpec((INPROJ_TILE, ATTN_WIDTH), row),
            pl.BlockSpec((1, nblk, ATTN_WIDTH), lambda i: (i, 0, 0)),
        ],
        out_shape=[
            jax.ShapeDtypeStruct((t, POOL_WIDTH), F32),
            jax.ShapeDtypeStruct((t, ATTN_WIDTH), BF16),
            jax.ShapeDtypeStruct((t // MOBA_BLOCK, ATTN_WIDTH, MOBA_BLOCK), BF16),
            jax.ShapeDtypeStruct((t, ATTN_WIDTH), BF16),
            jax.ShapeDtypeStruct((t // INPROJ_TILE, nblk, ATTN_WIDTH), F32),
        ],
        compiler_params=pltpu.CompilerParams(
            dimension_semantics=("arbitrary",),
            vmem_limit_bytes=VMEM_LIMIT_BYTES),
        name="inproj",
    )(h2d, g, w_qkv)


def _pool_branch(i, p_ref, halo_ref, poolw_ref, pscale_ref, pext_ref, ypool_ref, row0, tick):
    nq = MOBA_BLOCK
    halo = halo_ref[0]
    pext_ref[:MAX_POOL_HALO, :] = jnp.where(i > 0, halo, jnp.zeros_like(halo))
    pext_ref[MAX_POOL_HALO:, :] = p_ref[0]
    t_glob = i * nq + lax.broadcasted_iota(jnp.int32, (nq, POOL_GROUP_DIM), 0)
    for g, w in enumerate(POOL_WINDOWS):
        cols = slice(g * POOL_GROUP_DIM, (g + 1) * POOL_GROUP_DIM)
        xg = pext_ref[MAX_POOL_HALO:, cols]
        win = xg
        for s in range(1, w):
            win = win + pext_ref[MAX_POOL_HALO - s:MAX_POOL_HALO - s + nq, cols]
        count = jnp.minimum(t_glob + 1, w).astype(F32)
        mixed = (win / count - xg).astype(BF16)
        yg = _dot(mixed, poolw_ref[g]) * pscale_ref[:, cols]
        ypool_ref[row0:row0 + nq, cols] = yg.astype(BF16)
        tick()


def _augmented_queries(i, q, km_stack, qaug_ref, tile):
    nq = MOBA_BLOCK
    lane128 = lax.broadcasted_iota(jnp.int32, (nq, 128), 1)
    gt = _dot_nt(km_stack, q)
    gate = gt[:64, :] + gt[64:, :]
    g_n = [gate[n * 8:(n + 1) * 8, :] for n in range(N_BLOCKS)]
    slope = jnp.exp2(-(lax.broadcasted_iota(jnp.int32, (N_HEADS, nq), 0) + 1).astype(F32))
    r_lane = lax.broadcasted_iota(jnp.int32, (N_HEADS, nq), 1).astype(F32)
    pre = []
    for n in range(N_BLOCKS):
        cnt = jnp.zeros((N_HEADS, nq), F32)
        for n2 in range(N_BLOCKS):
            if n2 == n:
                continue
            beats = (g_n[n2] >= g_n[n]) if n2 < n else (g_n[n2] > g_n[n])
            cnt = cnt + jnp.where(jnp.logical_and(beats, n2 < i), 1.0, 0.0)
        sel = jnp.logical_and(cnt < MOBA_TOPK, n < i)
        dist = (i - n).astype(F32) * float(MOBA_BLOCK)
        bias = jnp.where(sel, -slope * dist, NEG)
        pre.append(jnp.where(n == i, 0.0, bias))
    pre.append(-slope * r_lane)
    pre.append(slope)
    pre.append(jnp.zeros((128 - 80, nq), F32))
    q_ext_all = jnp.concatenate(pre, axis=0).T
    for p in range(N_PAIRS):
        qp = q[:, p * 128:(p + 1) * 128].astype(F32)
        for half in range(2):
            h = 2 * p + half
            in_head = (lane128 >= HEAD_DIM) if half else (lane128 < HEAD_DIM)
            qh = jnp.where(in_head, qp, 0.0).astype(BF16)
            ext_lane = jnp.logical_or(
                jnp.logical_and(lane128 < 64, lane128 % 8 == h),
                jnp.logical_or(lane128 == 64 + h, lane128 == 72 + h))
            qe = jnp.where(ext_lane, q_ext_all, 0.0).astype(BF16)
            rows = slice(half * nq, (half + 1) * nq)
            qaug_ref[tile, p, rows, :128] = qh
            qaug_ref[tile, p, rows, 128:] = qe


def _mixer_kernel(hlo_ref, hhi_ref, plo_ref, phi_ref, halo_lo_ref, halo_hi_ref,
                  qlo_ref, qhi_ref, kt_ref, v_ref, km_ref,
                  gmix_ref, wgate_ref, poolw_ref, pscale_ref, wbp_ref, wba_ref, wout_ref,
                  olo_ref, ohi_ref,
                  pext_ref, qaug_ref, mxv_ref, m_ref, s_ref, acc_ref, attn_ref, ypool_ref,
                  u_ref, gates_ref):
    t = pl.program_id(1)
    nq = MOBA_BLOCK
    i_lo, i_hi = t, N_BLOCKS - 1 - t
    lane128 = lax.broadcasted_iota(jnp.int32, (nq, 128), 1)

    h1 = jnp.concatenate([hlo_ref[0], hhi_ref[0]], axis=0)
    u_ref[...] = _rms_norm(h1, gmix_ref[...]).astype(BF16)
    gate_chunks = iter(range(0, 2 * D_MODEL, GATE_CHUNK))

    def tick():
        c = next(gate_chunks)
        z = _dot(u_ref[...], wgate_ref[:, c:c + GATE_CHUNK])
        gates_ref[:, c:c + GATE_CHUNK] = 0.5 * jnp.tanh(0.5 * z) + 0.5

    _pool_branch(i_lo, plo_ref, halo_lo_ref, poolw_ref, pscale_ref, pext_ref, ypool_ref, 0, tick)
    _pool_branch(i_hi, phi_ref, halo_hi_ref, poolw_ref, pscale_ref, pext_ref, ypool_ref, nq, tick)
    y_pool = _dot(ypool_ref[...], wbp_ref[...])

    km = km_ref[0]
    head_of_lane = lax.broadcasted_iota(jnp.int32, (N_HEADS, ATTN_WIDTH), 1) // HEAD_DIM
    head_of_row = lax.broadcasted_iota(jnp.int32, (N_HEADS, ATTN_WIDTH), 0)
    head_mask = head_of_lane == head_of_row
    slabs = [jnp.where(head_mask, jnp.broadcast_to(km[n:n + 1, :], (N_HEADS, ATTN_WIDTH)), 0.0)
             for n in range(N_BLOCKS)]
    km_bd = jnp.concatenate(slabs, axis=0)
    km_hi = km_bd.astype(BF16)
    km_lo = (km_bd - km_hi.astype(F32)).astype(BF16)
    km_stack = jnp.concatenate([km_hi, km_lo], axis=0)
    _augmented_queries(i_lo, qlo_ref[0], km_stack, qaug_ref, 0)
    _augmented_queries(i_hi, qhi_ref[0], km_stack, qaug_ref, 1)

    ext_row = lax.broadcasted_iota(jnp.int32, (128, nq), 0)
    key_col = lax.broadcasted_iota(jnp.int32, (128, nq), 1)
    e_static = jnp.where(jnp.logical_and(ext_row >= 64, ext_row < 72), 1.0,
                         jnp.where(jnp.logical_and(ext_row >= 72, ext_row < 80),
                                   key_col.astype(F32), 0.0))
    ones_blk = jnp.ones((nq, 128), BF16)
    rr = lax.broadcasted_iota(jnp.int32, (2 * nq, nq), 0) % nq
    cc = lax.broadcasted_iota(jnp.int32, (2 * nq, nq), 1)
    causal_ok = cc <= rr

    n_dyn = N_BLOCKS // 2
    visits = []
    for v in range(N_VISITS):
        is_lo = v <= t
        tile = jnp.where(is_lo, 0, 1) if v < n_dyn else 1
        j = jnp.where(is_lo, v, v - t - 1)
        start = pl.multiple_of(j * nq, nq)
        e_j = jnp.where(jnp.logical_and(ext_row < 64, ext_row // 8 == j), 1.0,
                        e_static).astype(BF16)
        visits.append((is_lo, tile, start, e_j, j))

    def slot(p):
        return ((p // PAIR_GROUP) % 2) * PAIR_GROUP + p % PAIR_GROUP

    def score_visit(p, v):
        _, tile, _, e_j, j = visits[v]
        k_aug_t = jnp.concatenate([kt_ref[0, j, p * 128:(p + 1) * 128, :], e_j], axis=0)
        s = _dot(qaug_ref[tile, p], k_aug_t)
        if v == N_VISITS - 1:
            s = jnp.where(causal_ok, s, NEG)
        elif v < n_dyn:
            s = jnp.where(jnp.logical_or(causal_ok, v != t), s, NEG)
        s_ref[slot(p), v] = s
        fold = jnp.maximum(s[:, :128], s[:, 128:])
        if v <= n_dyn:
            mxv_ref[slot(p), v] = fold
        else:
            mxv_ref[slot(p), n_dyn] = jnp.maximum(mxv_ref[slot(p), n_dyn], fold)

    def row_max(p):
        buf = slot(p)
        m_hi = mxv_ref[buf, n_dyn]
        m_lo = jnp.full((2 * nq, 128), NEG, F32)
        for v in range(n_dyn):
            mv = mxv_ref[buf, v]
            m_lo = jnp.maximum(m_lo, jnp.where(visits[v][0], mv, NEG))
            m_hi = jnp.maximum(m_hi, jnp.where(visits[v][0], NEG, mv))
        m_ref[buf, 0] = jnp.broadcast_to(jnp.max(m_lo, axis=1, keepdims=True), (2 * nq, 128))
        m_ref[buf, 1] = jnp.broadcast_to(jnp.max(m_hi, axis=1, keepdims=True), (2 * nq, 128))

    def value_visit(p, v):
        buf = slot(p)
        _, tile, start, _, _ = visits[v]
        cols = slice(p * 128, (p + 1) * 128)
        m = m_ref[buf, tile]
        pexp = jnp.exp(s_ref[buf, v] - jnp.concatenate([m, m], axis=1)).astype(BF16)
        v_ext = jnp.concatenate([v_ref[0, pl.ds(start, nq), cols], ones_blk], axis=1)
        r = _dot(pexp, v_ext)
        run = r if v == 0 else acc_ref[buf, 1] + r
        if v < n_dyn:
            last_lo = v == t
            if v == 0:
                acc_ref[buf, 0] = run
            else:
                acc_ref[buf, 0] = jnp.where(last_lo, run, acc_ref[buf, 0])
            run = jnp.where(last_lo, 0.0, run)
        acc_ref[buf, 1] = run

    def normalise(p):
        for tile in range(2):
            acc = acc_ref[slot(p), tile]
            o_full = acc[:, :128] / acc[:, 128:]
            pair = jnp.where(lane128 < HEAD_DIM, o_full[:nq], o_full[nq:])
            attn_ref[tile * nq:(tile + 1) * nq, p * 128:(p + 1) * 128] = pair.astype(BF16)

    n_stages = N_PAIRS // PAIR_GROUP
    for stage in range(n_stages + 1):
        now = [stage * PAIR_GROUP + g for g in range(PAIR_GROUP)] if stage < n_stages else []
        before = [(stage - 1) * PAIR_GROUP + g for g in range(PAIR_GROUP)] if stage > 0 else []
        for v in range(N_VISITS):
            for p in now:
                score_visit(p, v)
            for p in before:
                value_visit(p, v)
        for p in before:
            normalise(p)
        for p in now:
            row_max(p)
    y_attn = _dot(attn_ref[...], wba_ref[...])

    merged = gates_ref[:, :D_MODEL] * y_pool + gates_ref[:, D_MODEL:] * y_attn
    out = h1 + _dot(merged.astype(BF16), wout_ref[...])
    olo_ref[0] = out[:nq]
    ohi_ref[0] = out[nq:]


def _mixer(h1, p, q, k, v, kmean, g_mix, w_gate, pool_w, pool_scale, w_bp, w_ba, w_out):
    b, s, d = h1.shape
    nq = MOBA_BLOCK
    nt = s // nq
    assert nt == N_BLOCKS
    halo_per_tile = nq // MAX_POOL_HALO
    lo = lambda bi, t: (bi, t, 0)
    hi = lambda bi, t: (bi, nt - 1 - t, 0)
    halo_lo = lambda bi, t: (bi, jnp.maximum(t * halo_per_tile - 1, 0), 0)
    halo_hi = lambda bi, t: (bi, (nt - 1 - t) * halo_per_tile - 1, 0)
    seq = lambda bi, t: (bi, 0, 0)
    c2 = lambda bi, t: (0, 0)
    c3 = lambda bi, t: (0, 0, 0)
    one = pl.Buffered(1)
    half = jax.ShapeDtypeStruct((b, s // 2, d), F32)
    return pl.pallas_call(
        _mixer_kernel,
        grid=(b, nt // 2),
        in_specs=[
            pl.BlockSpec((1, nq, d), lo),
            pl.BlockSpec((1, nq, d), hi),
            pl.BlockSpec((1, nq, POOL_WIDTH), lo),
            pl.BlockSpec((1, nq, POOL_WIDTH), hi),
            pl.BlockSpec((1, MAX_POOL_HALO, POOL_WIDTH), halo_lo),
            pl.BlockSpec((1, MAX_POOL_HALO, POOL_WIDTH), halo_hi),
            pl.BlockSpec((1, nq, ATTN_WIDTH), lo),
            pl.BlockSpec((1, nq, ATTN_WIDTH), hi),
            pl.BlockSpec((1, nt, ATTN_WIDTH, nq), lambda bi, t: (bi, 0, 0, 0)),
            pl.BlockSpec((1, s, ATTN_WIDTH), seq),
            pl.BlockSpec((1, nt, ATTN_WIDTH), seq),
            pl.BlockSpec((1, d), c2),
            pl.BlockSpec((d, 2 * d), c2, pipeline_mode=one),
            pl.BlockSpec((N_POOL_GROUPS, POOL_GROUP_DIM, POOL_GROUP_DIM), c3, pipeline_mode=one),
            pl.BlockSpec((1, POOL_WIDTH), c2),
            pl.BlockSpec((POOL_WIDTH, d), c2, pipeline_mode=one),
            pl.BlockSpec((ATTN_WIDTH, d), c2, pipeline_mode=one),
            pl.BlockSpec((d, d), c2, pipeline_mode=one),
        ],
        out_specs=[pl.BlockSpec((1, nq, d), lo), pl.BlockSpec((1, nq, d), lo)],
        out_shape=[half, half],
        scratch_shapes=[
            pltpu.VMEM((MAX_POOL_HALO + nq, POOL_WIDTH), F32),
            pltpu.VMEM((2, N_PAIRS, 2 * nq, 256), BF16),
            pltpu.VMEM((2 * PAIR_GROUP, N_BLOCKS // 2 + 1, 2 * nq, 128), F32),
            pltpu.VMEM((2 * PAIR_GROUP, 2, 2 * nq, 128), F32),
            pltpu.VMEM((2 * PAIR_GROUP, N_VISITS, 2 * nq, nq), F32),
            pltpu.VMEM((2 * PAIR_GROUP, 2, 2 * nq, 256), F32),
            pltpu.VMEM((2 * nq, ATTN_WIDTH), BF16),
            pltpu.VMEM((2 * nq, POOL_WIDTH), BF16),
            pltpu.VMEM((2 * nq, d), BF16),
            pltpu.VMEM((2 * nq, 2 * d), F32),
        ],
        compiler_params=pltpu.CompilerParams(
            dimension_semantics=("arbitrary", "arbitrary"),
            vmem_limit_bytes=VMEM_LIMIT_BYTES),
        name="mixer",
    )(h1, h1, p, p, p, p, q, q, k, v, kmean, g_mix, w_gate, pool_w, pool_scale,
      w_bp, w_ba, w_out)


def kernel(x, ffn1_norm, ffn1_w_gate, ffn1_w_up, ffn1_w_down, mix_norm, w_in,
           pool_w, pool_scale, w_branch_pool, w_branch_attn, w_out,
           ffn2_norm, ffn2_w_gate, ffn2_w_up, ffn2_w_down, final_norm):
    b, s, d = x.shape
    assert d == D_MODEL and s == N_BLOCKS * MOBA_BLOCK
    depth = ffn1_norm.shape[0]
    bf = lambda a: a.astype(BF16)
    h = x
    for l in range(depth):
        last = l == depth - 1
        h = _ffn(h.reshape(b * s, d), ffn1_norm[l][None], ffn1_w_gate[l],
                 ffn1_w_up[l], ffn1_w_down[l])
        w_l = bf(w_in[l])
        p, q, kt, v, kmean = _inproj(h, mix_norm[l][None], w_l[:, :QKV_WIDTH])
        shp = lambda a: a.reshape(b, s, a.shape[-1])
        h_lo, h_hi = _mixer(h.reshape(b, s, d), shp(p), shp(q),
                            kt.reshape(b, N_BLOCKS, ATTN_WIDTH, MOBA_BLOCK), shp(v),
                            kmean.reshape(b, N_BLOCKS, ATTN_WIDTH),
                            mix_norm[l][None], w_l[:, QKV_WIDTH:], bf(pool_w[l]),
                            pool_scale[l][None], bf(w_branch_pool[l]),
                            bf(w_branch_attn[l]), bf(w_out[l]))
        h = _ffn_paired(h_lo, h_hi, ffn2_norm[l][None], ffn2_w_gate[l],
                        ffn2_w_up[l], ffn2_w_down[l], final_norm[None], last)
    return h
```

```python
import functools

import jax
import jax.numpy as jnp
from jax import lax
from jax.experimental import pallas as pl
from jax.experimental.pallas import tpu as pltpu

EPS = 1e-6
D_MODEL = 1024
D_FF = 2816
POOL_WIDTH = 512
N_POOL_GROUPS = 4
POOL_GROUP_DIM = 128
POOL_WINDOWS = (2, 4, 8, 16)
ATTN_WIDTH = 512
N_HEADS = 8
HEAD_DIM = 64
N_PAIRS = N_HEADS // 2
MOBA_BLOCK = 256
MOBA_TOPK = 3
N_BLOCKS = 8
N_VISITS = N_BLOCKS + 1
PAIR_GROUP = 2
QKV_WIDTH = POOL_WIDTH + 3 * ATTN_WIDTH
NEG = -1e30

VMEM_LIMIT_BYTES = 58 * 1024 * 1024
FFN_TILE = N_BLOCKS * MOBA_BLOCK
FFN_SUB = 256
FFN_CHUNKS = ((0, 1024), (1024, 1024), (2048, 768))
INPROJ_TILE = 1024
MAX_POOL_HALO = 16
GATE_CHUNK = 2 * D_MODEL // (2 * N_POOL_GROUPS)

BF16 = jnp.bfloat16
F32 = jnp.float32


def _rms_norm(x, g):
    ms = jnp.mean(x * x, axis=-1, keepdims=True)
    return x * lax.rsqrt(ms + EPS) * g


def _dot(a, b):
    return jnp.dot(a, b, preferred_element_type=F32)


def _dot_nt(a, b):
    return lax.dot_general(a, b, (((1,), (1,)), ((), ())),
                           preferred_element_type=F32)


def _ffn_rows(xs, g_ref, wg_ref, wu_ref, wd_ref, gf_ref, store):
    def norm(x):
        return _rms_norm(x, g_ref[...]).astype(BF16)

    def finish(n, acc):
        h = xs[n] + 0.5 * acc
        if gf_ref is not None:
            h = _rms_norm(h, gf_ref[...])
        store(n, h)

    xns = {0: norm(xs[0])}

    def gate_up(n, ci):
        start, size = FFN_CHUNKS[ci]
        return (_dot(xns[n], wg_ref[:, start:start + size]),
                _dot(xns[n], wu_ref[:, start:start + size]))

    items = [(n, ci) for n in range(len(xs)) for ci in range(len(FFN_CHUNKS))]
    ab = gate_up(*items[0])
    acc = None
    pending = None
    for k, (n, ci) in enumerate(items):
        if ci == 0 and n + 1 < len(xs):
            xns[n + 1] = norm(xs[n + 1])
        ab_next = gate_up(*items[k + 1]) if k + 1 < len(items) else None
        if ci == 0 and pending is not None:
            finish(*pending)
        a, b = ab
        start, size = FFN_CHUNKS[ci]
        hmid = (a * jax.nn.sigmoid(a) * b).astype(BF16)
        d = _dot(hmid, wd_ref[start:start + size, :])
        acc = d if ci == 0 else acc + d
        if ci == len(FFN_CHUNKS) - 1:
            pending = (n, acc)
        ab = ab_next
    finish(*pending)


def _ffn_kernel(x_ref, g_ref, wg_ref, wu_ref, wd_ref, o_ref):
    groups = [slice(n * FFN_SUB, (n + 1) * FFN_SUB) for n in range(FFN_TILE // FFN_SUB)]

    def store(n, h):
        o_ref[groups[n], :] = h

    _ffn_rows([x_ref[rows, :] for rows in groups], g_ref, wg_ref, wu_ref, wd_ref, None, store)


def _ffn_weight_specs(d, const):
    one = pl.Buffered(1)
    return [
        pl.BlockSpec((1, d), const),
        pl.BlockSpec((d, D_FF), const, pipeline_mode=one),
        pl.BlockSpec((d, D_FF), const, pipeline_mode=one),
        pl.BlockSpec((D_FF, d), const, pipeline_mode=one),
    ]


def _ffn(x2d, g, wg, wu, wd):
    t, d = x2d.shape
    return pl.pallas_call(
        _ffn_kernel,
        grid=(t // FFN_TILE,),
        in_specs=[pl.BlockSpec((FFN_TILE, d), lambda i: (i, 0))]
        + _ffn_weight_specs(d, lambda i: (0, 0)),
        out_specs=pl.BlockSpec((FFN_TILE, d), lambda i: (i, 0)),
        out_shape=jax.ShapeDtypeStruct((t, d), F32),
        compiler_params=pltpu.CompilerParams(
            dimension_semantics=("arbitrary",),
            vmem_limit_bytes=VMEM_LIMIT_BYTES),
        name="ffn",
    )(x2d, g, wg, wu, wd)


def _ffn_paired_kernel(lo_ref, hi_ref, g_ref, wg_ref, wu_ref, wd_ref, gf_ref, o_ref,
                       *, final_norm):
    half_s = lo_ref.shape[1]
    n_hi_tiles = half_s // MOBA_BLOCK
    per_group = FFN_SUB // MOBA_BLOCK
    xs = [lo_ref[0, n * FFN_SUB:(n + 1) * FFN_SUB, :] for n in range(half_s // FFN_SUB)]
    for n in range(half_s // FFN_SUB):
        tiles = [n_hi_tiles - 1 - (n * per_group + k) for k in range(per_group)]
        xs.append(jnp.concatenate(
            [hi_ref[0, i * MOBA_BLOCK:(i + 1) * MOBA_BLOCK, :] for i in tiles], axis=0))

    def store(n, h):
        o_ref[0, n * FFN_SUB:(n + 1) * FFN_SUB, :] = h

    _ffn_rows(xs, g_ref, wg_ref, wu_ref, wd_ref, gf_ref if final_norm else None, store)


def _ffn_paired(h_lo, h_hi, g, wg, wu, wd, gf, final_norm):
    b, half_s, d = h_lo.shape
    assert 2 * half_s == FFN_TILE and half_s % FFN_SUB == 0
    const = lambda bi: (0, 0)
    seq = lambda bi: (bi, 0, 0)
    return pl.pallas_call(
        functools.partial(_ffn_paired_kernel, final_norm=final_norm),
        grid=(b,),
        in_specs=[pl.BlockSpec((1, half_s, d), seq), pl.BlockSpec((1, half_s, d), seq)]
        + _ffn_weight_specs(d, const) + [pl.BlockSpec((1, d), const)],
        out_specs=pl.BlockSpec((1, FFN_TILE, d), seq),
        out_shape=jax.ShapeDtypeStruct((b, FFN_TILE, d), F32),
        compiler_params=pltpu.CompilerParams(
            dimension_semantics=("arbitrary",),
            vmem_limit_bytes=VMEM_LIMIT_BYTES),
        name="ffn_final" if final_norm else "ffn_paired",
    )(h_lo, h_hi, g, wg, wu, wd, gf)


def _inproj_kernel(h_ref, g_ref, w_ref, p_ref, q_ref, kt_ref, v_ref, km_ref):
    u = _rms_norm(h_ref[...], g_ref[...]).astype(BF16)
    proj = _dot(u, w_ref[...])
    p_ref[...] = proj[:, :POOL_WIDTH]
    o = POOL_WIDTH
    q_ref[...] = (proj[:, o:o + ATTN_WIDTH] * (HEAD_DIM ** -0.5)).astype(BF16)
    k = proj[:, o + ATTN_WIDTH:o + 2 * ATTN_WIDTH]
    v_ref[...] = proj[:, o + 2 * ATTN_WIDTH:o + 3 * ATTN_WIDTH].astype(BF16)
    for blk in range(INPROJ_TILE // MOBA_BLOCK):
        kb = k[blk * MOBA_BLOCK:(blk + 1) * MOBA_BLOCK, :]
        kt_ref[blk] = kb.T.astype(BF16)
        km_ref[0, blk:blk + 1, :] = jnp.mean(kb, axis=0, keepdims=True)


def _inproj(h2d, g, w_qkv):
    t, d = h2d.shape
    nblk = INPROJ_TILE // MOBA_BLOCK
    row = lambda i: (i, 0)
    const = lambda i: (0, 0)
    return pl.pallas_call(
        _inproj_kernel,
        grid=(t // INPROJ_TILE,),
        in_specs=[
            pl.BlockSpec((INPROJ_TILE, d), row),
            pl.BlockSpec((1, d), const),
            pl.BlockSpec((d, QKV_WIDTH), const, pipeline_mode=pl.Buffered(1)),
        ],
        out_specs=[
            pl.BlockSpec((INPROJ_TILE, POOL_WIDTH), row),
            pl.BlockSpec((INPROJ_TILE, ATTN_WIDTH), row),
            pl.BlockSpec((nblk, ATTN_WIDTH, MOBA_BLOCK), lambda i: (i, 0, 0)),
            pl.BlockSpec((INPROJ_TILE, ATTN_WIDTH), row),
            pl.BlockSpec((1, nblk, ATTN_WIDTH), lambda i: (i, 0, 0)),
        ],
        out_shape=[
            jax.ShapeDtypeStruct((t, POOL_WIDTH), F32),
            jax.ShapeDtypeStruct((t, ATTN_WIDTH), BF16),
            jax.ShapeDtypeStruct((t // MOBA_BLOCK, ATTN_WIDTH, MOBA_BLOCK), BF16),
            jax.ShapeDtypeStruct((t, ATTN_WIDTH), BF16),
            jax.ShapeDtypeStruct((t // INPROJ_TILE, nblk, ATTN_WIDTH), F32),
        ],
        compiler_params=pltpu.CompilerParams(
            dimension_semantics=("arbitrary",),
            vmem_limit_bytes=VMEM_LIMIT_BYTES),
        name="inproj",
    )(h2d, g, w_qkv)


def _pool_branch(i, p_ref, halo_ref, poolw_ref, pscale_ref, pext_ref, ypool_ref, row0, tick):
    nq = MOBA_BLOCK
    halo = halo_ref[0]
    pext_ref[:MAX_POOL_HALO, :] = jnp.where(i > 0, halo, jnp.zeros_like(halo))
    pext_ref[MAX_POOL_HALO:, :] = p_ref[0]
    t_glob = i * nq + lax.broadcasted_iota(jnp.int32, (nq, POOL_GROUP_DIM), 0)
    for g, w in enumerate(POOL_WINDOWS):
        cols = slice(g * POOL_GROUP_DIM, (g + 1) * POOL_GROUP_DIM)
        xg = pext_ref[MAX_POOL_HALO:, cols]
        win = xg
        for s in range(1, w):
            win = win + pext_ref[MAX_POOL_HALO - s:MAX_POOL_HALO - s + nq, cols]
        count = jnp.minimum(t_glob + 1, w).astype(F32)
        mixed = (win / count - xg).astype(BF16)
        yg = _dot(mixed, poolw_ref[g]) * pscale_ref[:, cols]
        ypool_ref[row0:row0 + nq, cols] = yg.astype(BF16)
        tick()


def _augmented_queries(i, q, km_stack, qaug_ref, tile):
    nq = MOBA_BLOCK
    lane128 = lax.broadcasted_iota(jnp.int32, (nq, 128), 1)
    gt = _dot_nt(km_stack, q)
    gate = gt[:64, :] + gt[64:, :]
    g_n = [gate[n * 8:(n + 1) * 8, :] for n in range(N_BLOCKS)]
    slope = jnp.exp2(-(lax.broadcasted_iota(jnp.int32, (N_HEADS, nq), 0) + 1).astype(F32))
    r_lane = lax.broadcasted_iota(jnp.int32, (N_HEADS, nq), 1).astype(F32)
    pre = []
    for n in range(N_BLOCKS):
        cnt = jnp.zeros((N_HEADS, nq), F32)
        for n2 in range(N_BLOCKS):
            if n2 == n:
                continue
            beats = (g_n[n2] >= g_n[n]) if n2 < n else (g_n[n2] > g_n[n])
            cnt = cnt + jnp.where(jnp.logical_and(beats, n2 < i), 1.0, 0.0)
        sel = jnp.logical_and(cnt < MOBA_TOPK, n < i)
        dist = (i - n).astype(F32) * float(MOBA_BLOCK)
        bias = jnp.where(sel, -slope * dist, NEG)
        pre.append(jnp.where(n == i, 0.0, bias))
    pre.append(-slope * r_lane)
    pre.append(slope)
    pre.append(jnp.zeros((128 - 80, nq), F32))
    q_ext_all = jnp.concatenate(pre, axis=0).T
    for p in range(N_PAIRS):
        qp = q[:, p * 128:(p + 1) * 128].astype(F32)
        for half in range(2):
            h = 2 * p + half
            in_head = (lane128 >= HEAD_DIM) if half else (lane128 < HEAD_DIM)
            qh = jnp.where(in_head, qp, 0.0).astype(BF16)
            ext_lane = jnp.logical_or(
                jnp.logical_and(lane128 < 64, lane128 % 8 == h),
                jnp.logical_or(lane128 == 64 + h, lane128 == 72 + h))
            qe = jnp.where(ext_lane, q_ext_all, 0.0).astype(BF16)
            rows = slice(half * nq, (half + 1) * nq)
            qaug_ref[tile, p, rows, :128] = qh
            qaug_ref[tile, p, rows, 128:] = qe


def _mixer_kernel(hlo_ref, hhi_ref, plo_ref, phi_ref, halo_lo_ref, halo_hi_ref,
                  qlo_ref, qhi_ref, kt_ref, v_ref, km_ref,
                  gmix_ref, wgate_ref, poolw_ref, pscale_ref, wbp_ref, wba_ref, wout_ref,
                  olo_ref, ohi_ref,
                  pext_ref, qaug_ref, mxv_ref, m_ref, s_ref, acc_ref, attn_ref, ypool_ref,
                  u_ref, gates_ref):
    t = pl.program_id(1)
    nq = MOBA_BLOCK
    i_lo, i_hi = t, N_BLOCKS - 1 - t
    lane128 = lax.broadcasted_iota(jnp.int32, (nq, 128), 1)

    h1 = jnp.concatenate([hlo_ref[0], hhi_ref[0]], axis=0)
    u_ref[...] = _rms_norm(h1, gmix_ref[...]).astype(BF16)
    gate_chunks = iter(range(0, 2 * D_MODEL, GATE_CHUNK))

    def tick():
        c = next(gate_chunks)
        z = _dot(u_ref[...], wgate_ref[:, c:c + GATE_CHUNK])
        gates_ref[:, c:c + GATE_CHUNK] = 0.5 * jnp.tanh(0.5 * z) + 0.5

    _pool_branch(i_lo, plo_ref, halo_lo_ref, poolw_ref, pscale_ref, pext_ref, ypool_ref, 0, tick)
    _pool_branch(i_hi, phi_ref, halo_hi_ref, poolw_ref, pscale_ref, pext_ref, ypool_ref, nq, tick)
    y_pool = _dot(ypool_ref[...], wbp_ref[...])

    km = km_ref[0]
    head_of_lane = lax.broadcasted_iota(jnp.int32, (N_HEADS, ATTN_WIDTH), 1) // HEAD_DIM
    head_of_row = lax.broadcasted_iota(jnp.int32, (N_HEADS, ATTN_WIDTH), 0)
    head_mask = head_of_lane == head_of_row
    slabs = [jnp.where(head_mask, jnp.broadcast_to(km[n:n + 1, :], (N_HEADS, ATTN_WIDTH)), 0.0)
             for n in range(N_BLOCKS)]
    km_bd = jnp.concatenate(slabs, axis=0)
    km_hi = km_bd.astype(BF16)
    km_lo = (km_bd - km_hi.astype(F32)).astype(BF16)
    km_stack = jnp.concatenate([km_hi, km_lo], axis=0)
    _augmented_queries(i_lo, qlo_ref[0], km_stack, qaug_ref, 0)
    _augmented_queries(i_hi, qhi_ref[0], km_stack, qaug_ref, 1)

    ext_row = lax.broadcasted_iota(jnp.int32, (128, nq), 0)
    key_col = lax.broadcasted_iota(jnp.int32, (128, nq), 1)
    e_static = jnp.where(jnp.logical_and(ext_row >= 64, ext_row < 72), 1.0,
                         jnp.where(jnp.logical_and(ext_row >= 72, ext_row < 80),
                                   key_col.astype(F32), 0.0))
    ones_blk = jnp.ones((nq, 128), BF16)
    rr = lax.broadcasted_iota(jnp.int32, (2 * nq, nq), 0) % nq
    cc = lax.broadcasted_iota(jnp.int32, (2 * nq, nq), 1)
    causal_ok = cc <= rr

    n_dyn = N_BLOCKS // 2
    visits = []
    for v in range(N_VISITS):
        is_lo = v <= t
        tile = jnp.where(is_lo, 0, 1) if v < n_dyn else 1
        j = jnp.where(is_lo, v, v - t - 1)
        start = pl.multiple_of(j * nq, nq)
        e_j = jnp.where(jnp.logical_and(ext_row < 64, ext_row // 8 == j), 1.0,
                        e_static).astype(BF16)
        visits.append((is_lo, tile, start, e_j, j))

    def slot(p):
        return ((p // PAIR_GROUP) % 2) * PAIR_GROUP + p % PAIR_GROUP

    def score_visit(p, v):
        _, tile, _, e_j, j = visits[v]
        k_aug_t = jnp.concatenate([kt_ref[0, j, p * 128:(p + 1) * 128, :], e_j], axis=0)
        s = _dot(qaug_ref[tile, p], k_aug_t)
        if v == N_VISITS - 1:
            s = jnp.where(causal_ok, s, NEG)
        elif v < n_dyn:
            s = jnp.where(jnp.logical_or(causal_ok, v != t), s, NEG)
        s_ref[slot(p), v] = s
        fold = jnp.maximum(s[:, :128], s[:, 128:])
        if v <= n_dyn:
            mxv_ref[slot(p), v] = fold
        else:
            mxv_ref[slot(p), n_dyn] = jnp.maximum(mxv_ref[slot(p), n_dyn], fold)

    def row_max(p):
        buf = slot(p)
        m_hi = mxv_ref[buf, n_dyn]
        m_lo = jnp.full((2 * nq, 128), NEG, F32)
        for v in range(n_dyn):
            mv = mxv_ref[buf, v]
            m_lo = jnp.maximum(m_lo, jnp.where(visits[v][0], mv, NEG))
            m_hi = jnp.maximum(m_hi, jnp.where(visits[v][0], NEG, mv))
        m_ref[buf, 0] = jnp.broadcast_to(jnp.max(m_lo, axis=1, keepdims=True), (2 * nq, 128))
        m_ref[buf, 1] = jnp.broadcast_to(jnp.max(m_hi, axis=1, keepdims=True), (2 * nq, 128))

    def value_visit(p, v):
        buf = slot(p)
        _, tile, start, _, _ = visits[v]
        cols = slice(p * 128, (p + 1) * 128)
        m = m_ref[buf, tile]
        pexp = jnp.exp(s_ref[buf, v] - jnp.concatenate([m, m], axis=1)).astype(BF16)
        v_ext = jnp.concatenate([v_ref[0, pl.ds(start, nq), cols], ones_blk], axis=1)
        r = _dot(pexp, v_ext)
        run = r if v == 0 else acc_ref[buf, 1] + r
        if v < n_dyn:
            last_lo = v == t
            if v == 0:
                acc_ref[buf, 0] = run
            else:
                acc_ref[buf, 0] = jnp.where(last_lo, run, acc_ref[buf, 0])
            run = jnp.where(last_lo, 0.0, run)
        acc_ref[buf, 1] = run

    def normalise(p):
        for tile in range(2):
            acc = acc_ref[slot(p), tile]
            o_full = acc[:, :128] / acc[:, 128:]
            pair = jnp.where(lane128 < HEAD_DIM, o_full[:nq], o_full[nq:])
            attn_ref[tile * nq:(tile + 1) * nq, p * 128:(p + 1) * 128] = pair.astype(BF16)

    n_stages = N_PAIRS // PAIR_GROUP
    for stage in range(n_stages + 1):
        now = [stage * PAIR_GROUP + g for g in range(PAIR_GROUP)] if stage < n_stages else []
        before = [(stage - 1) * PAIR_GROUP + g for g in range(PAIR_GROUP)] if stage > 0 else []
        for v in range(N_VISITS):
            for p in now:
                score_visit(p, v)
            for p in before:
                value_visit(p, v)
        for p in before:
            normalise(p)
        for p in now:
            row_max(p)
    y_attn = _dot(attn_ref[...], wba_ref[...])

    merged = gates_ref[:, :D_MODEL] * y_pool + gates_ref[:, D_MODEL:] * y_attn
    out = h1 + _dot(merged.astype(BF16), wout_ref[...])
    olo_ref[0] = out[:nq]
    ohi_ref[0] = out[nq:]


def _mixer(h1, p, q, k, v, kmean, g_mix, w_gate, pool_w, pool_scale, w_bp, w_ba, w_out):
    b, s, d = h1.shape
    nq = MOBA_BLOCK
    nt = s // nq
    assert nt == N_BLOCKS
    halo_per_tile = nq // MAX_POOL_HALO
    lo = lambda bi, t: (bi, t, 0)
    hi = lambda bi, t: (bi, nt - 1 - t, 0)
    halo_lo = lambda bi, t: (bi, jnp.maximum(t * halo_per_tile - 1, 0), 0)
    halo_hi = lambda bi, t: (bi, (nt - 1 - t) * halo_per_tile - 1, 0)
    seq = lambda bi, t: (bi, 0, 0)
    c2 = lambda bi, t: (0, 0)
    c3 = lambda bi, t: (0, 0, 0)
    one = pl.Buffered(1)
    half = jax.ShapeDtypeStruct((b, s // 2, d), F32)
    return pl.pallas_call(
        _mixer_kernel,
        grid=(b, nt // 2),
        in_specs=[
            pl.BlockSpec((1, nq, d), lo),
            pl.BlockSpec((1, nq, d), hi),
            pl.BlockSpec((1, nq, POOL_WIDTH), lo),
            pl.BlockSpec((1, nq, POOL_WIDTH), hi),
            pl.BlockSpec((1, MAX_POOL_HALO, POOL_WIDTH), halo_lo),
            pl.BlockSpec((1, MAX_POOL_HALO, POOL_WIDTH), halo_hi),
            pl.BlockSpec((1, nq, ATTN_WIDTH), lo),
            pl.BlockSpec((1, nq, ATTN_WIDTH), hi),
            pl.BlockSpec((1, nt, ATTN_WIDTH, nq), lambda bi, t: (bi, 0, 0, 0)),
            pl.BlockSpec((1, s, ATTN_WIDTH), seq),
            pl.BlockSpec((1, nt, ATTN_WIDTH), seq),
            pl.BlockSpec((1, d), c2),
            pl.BlockSpec((d, 2 * d), c2, pipeline_mode=one),
            pl.BlockSpec((N_POOL_GROUPS, POOL_GROUP_DIM, POOL_GROUP_DIM), c3, pipeline_mode=one),
            pl.BlockSpec((1, POOL_WIDTH), c2),
            pl.BlockSpec((POOL_WIDTH, d), c2, pipeline_mode=one),
            pl.BlockSpec((ATTN_WIDTH, d), c2, pipeline_mode=one),
            pl.BlockSpec((d, d), c2, pipeline_mode=one),
        ],
        out_specs=[pl.BlockSpec((1, nq, d), lo), pl.BlockSpec((1, nq, d), lo)],
        out_shape=[half, half],
        scratch_shapes=[
            pltpu.VMEM((MAX_POOL_HALO + nq, POOL_WIDTH), F32),
            pltpu.VMEM((2, N_PAIRS, 2 * nq, 256), BF16),
            pltpu.VMEM((2 * PAIR_GROUP, N_BLOCKS // 2 + 1, 2 * nq, 128), F32),
            pltpu.VMEM((2 * PAIR_GROUP, 2, 2 * nq, 128), F32),
            pltpu.VMEM((2 * PAIR_GROUP, N_VISITS, 2 * nq, nq), F32),
            pltpu.VMEM((2 * PAIR_GROUP, 2, 2 * nq, 256), F32),
            pltpu.VMEM((2 * nq, ATTN_WIDTH), BF16),
            pltpu.VMEM((2 * nq, POOL_WIDTH), BF16),
            pltpu.VMEM((2 * nq, d), BF16),
            pltpu.VMEM((2 * nq, 2 * d), F32),
        ],
        compiler_params=pltpu.CompilerParams(
            dimension_semantics=("arbitrary", "arbitrary"),
            vmem_limit_bytes=VMEM_LIMIT_BYTES),
        name="mixer",
    )(h1, h1, p, p, p, p, q, q, k, v, kmean, g_mix, w_gate, pool_w, pool_scale,
      w_bp, w_ba, w_out)


def kernel(x, ffn1_norm, ffn1_w_gate, ffn1_w_up, ffn1_w_down, mix_norm, w_in,
           pool_w, pool_scale, w_branch_pool, w_branch_attn, w_out,
           ffn2_norm, ffn2_w_gate, ffn2_w_up, ffn2_w_down, final_norm):
    b, s, d = x.shape
    assert d == D_MODEL and s == N_BLOCKS * MOBA_BLOCK
    depth = ffn1_norm.shape[0]
    bf = lambda a: a.astype(BF16)
    h = x
    for l in range(depth):
        last = l == depth - 1
        h = _ffn(h.reshape(b * s, d), ffn1_norm[l][None], bf(ffn1_w_gate[l]),
                 bf(ffn1_w_up[l]), bf(ffn1_w_down[l]))
        w_l = bf(w_in[l])
        p, q, kt, v, kmean = _inproj(h, mix_norm[l][None], w_l[:, :QKV_WIDTH])
        shp = lambda a: a.reshape(b, s, a.shape[-1])
        h_lo, h_hi = _mixer(h.reshape(b, s, d), shp(p), shp(q),
                            kt.reshape(b, N_BLOCKS, ATTN_WIDTH, MOBA_BLOCK), shp(v),
                            kmean.reshape(b, N_BLOCKS, ATTN_WIDTH),
                            mix_norm[l][None], w_l[:, QKV_WIDTH:], bf(pool_w[l]),
                            pool_scale[l][None], bf(w_branch_pool[l]),
                            bf(w_branch_attn[l]), bf(w_out[l]))
        h = _ffn_paired(h_lo, h_hi, ffn2_norm[l][None], bf(ffn2_w_gate[l]),
                        bf(ffn2_w_up[l]), bf(ffn2_w_down[l]), final_norm[None], last)
    return h
```
